```python
import jax
import jax.numpy as jnp
from jax import lax
import numpy as np

D_MODEL = 1024
BATCH = 8
SEQ = 4096
DEPTH = 2

CTX_LEN = 256
GRID_W = 64
EPS = 1e-6
N_MOD = 9

GROUP_W = 256
MIX_W = 4 * GROUP_W

FNET_HEADS = 4
FNET_HD = GROUP_W // FNET_HEADS

MLA_HEADS = 4
QK_NOPE = 64
QK_ROPE = 32
AXIS_ROPE = QK_ROPE // 2
V_HD = 64
Q_RANK = 192
KV_RANK = 128
ROPE_BASE = 10000.0
Q_BLOCK = 128

SGU_HEADS = 4
SGU_HD = GROUP_W // SGU_HEADS
SGU_CHUNK = 128

POOL_WINDOWS = (2, 4, 8, 16)
POOL_GROUPS = len(POOL_WINDOWS)
POOL_HD = GROUP_W // POOL_GROUPS

D_FF = 2816

OFF_F = 0
OFF_Q = OFF_F + GROUP_W
OFF_KV = OFF_Q + Q_RANK
OFF_KR = OFF_KV + KV_RANK
OFF_G = OFF_KR + QK_ROPE
OFF_P = OFF_G + 2 * GROUP_W
IN_W = OFF_P + GROUP_W

kernel_name = "hybrid_parallel_group_diffusion_block"


def rms_norm(x, g):
    xf = x.astype(jnp.float32)
    y = xf * lax.rsqrt(jnp.mean(xf * xf, axis=-1, keepdims=True) + EPS)
    return (y * g.astype(jnp.float32)).astype(x.dtype)


def modulate(x, shift, scale):
    return x * (1 + scale) + shift


def swiglu(x, w13, w2):
    a, b = jnp.split(x @ w13, 2, axis=-1)
    return (jax.nn.silu(a) * b) @ w2


def ffn_half_step(h, g, shift, scale, gate, w13, w2):
    return h + 0.5 * gate * swiglu(modulate(rms_norm(h, g), shift, scale), w13, w2)


def axial_rope_tables(n_tokens):
    rows = n_tokens // GRID_W
    row = jnp.repeat(jnp.arange(rows, dtype=jnp.float32), GRID_W)
    col = (jnp.arange(n_tokens) % GRID_W).astype(jnp.float32)
    inv = jnp.power(ROPE_BASE, -jnp.arange(0, AXIS_ROPE, 2, dtype=jnp.float32) / AXIS_ROPE)
    ang = jnp.stack([row[:, None] * inv, col[:, None] * inv], axis=1)
    return jnp.cos(ang), jnp.sin(ang)


def apply_axial_rope(x, cos, sin):
    xs = x.reshape(x.shape[:-1] + (2, 2, AXIS_ROPE // 2)).astype(jnp.float32)
    x1, x2 = xs[..., 0, :], xs[..., 1, :]
    c = cos[None, :, None]
    s = sin[None, :, None]
    out = jnp.stack([x1 * c - x2 * s, x1 * s + x2 * c], axis=-2)
    return out.reshape(x.shape).astype(x.dtype)


def fourier_mixer(z, w_f):
    B, L, _ = z.shape
    zh = z.reshape(B, L, FNET_HEADS, FNET_HD).astype(jnp.float32)
    f = jnp.fft.fft2(zh, axes=(1, 3), norm="ortho").real.astype(z.dtype)
    return jnp.einsum('blhc,hcd->blhd', f, w_f).reshape(B, L, GROUP_W)


def mla_query(cq, g_q, w_uq, rope):
    B, L, _ = cq.shape
    q = (rms_norm(cq, g_q) @ w_uq).reshape(B, L, MLA_HEADS, QK_NOPE + QK_ROPE)
    if rope is not None:
        q = jnp.concatenate([q[..., :QK_NOPE], apply_axial_rope(q[..., QK_NOPE:], *rope)], axis=-1)
    return q


def mla_keys_values(ckv, kr, g_kv, w_ukv, rope):
    B, L, _ = ckv.shape
    kv = (rms_norm(ckv, g_kv) @ w_ukv).reshape(B, L, MLA_HEADS, QK_NOPE + V_HD)
    k_rope = kr[:, :, None, :]
    if rope is not None:
        k_rope = apply_axial_rope(k_rope, *rope)
    k = jnp.concatenate([kv[..., :QK_NOPE], jnp.broadcast_to(k_rope, (B, L, MLA_HEADS, QK_ROPE))], axis=-1)
    return k, kv[..., QK_NOPE:]


def attend(q, k, v):
    s = jnp.einsum('bqhd,bkhd->bhqk', q, k).astype(jnp.float32) * (QK_NOPE + QK_ROPE) ** -0.5
    p = jax.nn.softmax(s, axis=-1).astype(v.dtype)
    return jnp.einsum('bhqk,bkhd->bqhd', p, v)


def blocked_attention(q, k, v):
    B, L, H, dk = q.shape
    qb = q.reshape(B, L // Q_BLOCK, Q_BLOCK, H, dk).transpose(1, 0, 2, 3, 4)
    ob = lax.map(lambda qi: attend(qi, k, v), qb)
    return ob.transpose(1, 0, 2, 3, 4).reshape(B, L, H, v.shape[-1])


def spatial_gating_mixer(z, g_v, w_s, b_s):
    B, L, _ = z.shape
    z = jax.nn.gelu(z)
    u, v = jnp.split(z, 2, axis=-1)
    v = rms_norm(v.reshape(B, L, SGU_HEADS, SGU_HD), g_v)
    v = v.reshape(B, L // SGU_CHUNK, SGU_CHUNK, SGU_HEADS, SGU_HD)
    v = jnp.einsum('hpq,bnqhc->bnphc', w_s, v) + b_s.T[None, None, :, :, None]
    return u * v.reshape(B, L, GROUP_W)


def pooling_mixer(z, w_p, s_p):
    B, L, _ = z.shape
    zf = z.reshape(B, L, POOL_GROUPS, POOL_HD).astype(jnp.float32)
    cs = jnp.concatenate([jnp.zeros((B, 1, POOL_GROUPS, POOL_HD), jnp.float32), jnp.cumsum(zf, axis=1)], axis=1)
    t = jnp.arange(L)[:, None]
    w = jnp.array(POOL_WINDOWS, dtype=jnp.int32)[None, :]
    lo = jnp.clip(t - w // 2, 0, L)
    hi = jnp.clip(t - w // 2 + w, 0, L)
    gi = jnp.arange(POOL_GROUPS)[None, :]
    win_sum = cs[:, hi, gi] - cs[:, lo, gi]
    pooled = win_sum / (hi - lo).astype(jnp.float32)[None, :, :, None] - zf
    y = jnp.einsum('blgc,gcd->blgd', pooled.astype(z.dtype), w_p).reshape(B, L, GROUP_W)
    return y * s_p


def mix_heads(z, k, v, rope, w_fnet, g_q, w_uq, g_sgu, w_sgu, b_sgu, w_pool, s_pool):
    B, L, _ = z.shape
    q = mla_query(z[..., OFF_Q:OFF_KV], g_q, w_uq, rope)
    att = blocked_attention(q, k, v).reshape(B, L, GROUP_W)
    return jnp.concatenate([
        fourier_mixer(z[..., OFF_F:OFF_Q], w_fnet),
        att,
        spatial_gating_mixer(z[..., OFF_G:OFF_P], g_sgu, w_sgu, b_sgu),
        pooling_mixer(z[..., OFF_P:], w_pool, s_pool),
    ], axis=-1)


def setup_inputs(seed: int = 0) -> dict:
    key = jax.random.key(seed)
    ks = iter(jax.random.split(key, 32))
    f32 = jnp.float32

    def nrm(shape, scale):
        return jax.random.normal(next(ks), shape, f32) * scale

    def gain(shape):
        return 1.0 + 0.1 * jax.random.normal(next(ks), shape, f32)

    n = DEPTH
    return {
        "x": nrm((BATCH, SEQ, D_MODEL), 1.0),
        "c": nrm((BATCH, D_MODEL), 1.0),
        "ctx": nrm((BATCH, CTX_LEN, D_MODEL), 1.0),
        "c_ctx": nrm((D_MODEL,), 1.0),
        "w_ada": nrm((n, D_MODEL, N_MOD * D_MODEL), 0.5 * D_MODEL ** -0.5),
        "b_ada": nrm((n, N_MOD * D_MODEL), 0.02),
        "g_ffn1": gain((n, D_MODEL)),
        "w13_ffn1": nrm((n, D_MODEL, 2 * D_FF), D_MODEL ** -0.5),
        "w2_ffn1": nrm((n, D_FF, D_MODEL), D_FF ** -0.5),
        "g_mix": gain((n, D_MODEL)),
        "w_in": nrm((n, D_MODEL, IN_W), D_MODEL ** -0.5),
        "w_fnet": nrm((n, FNET_HEADS, FNET_HD, FNET_HD), FNET_HD ** -0.5),
        "g_q": gain((n, Q_RANK)),
        "w_uq": nrm((n, Q_RANK, MLA_HEADS * (QK_NOPE + QK_ROPE)), Q_RANK ** -0.5),
        "g_kv": gain((n, KV_RANK)),
        "w_ukv": nrm((n, KV_RANK, MLA_HEADS * (QK_NOPE + V_HD)), KV_RANK ** -0.5),
        "g_sgu": gain((n, SGU_HEADS, SGU_HD)),
        "w_sgu": nrm((n, SGU_HEADS, SGU_CHUNK, SGU_CHUNK), SGU_CHUNK ** -0.5),
        "b_sgu": gain((n, SGU_HEADS, SGU_CHUNK)),
        "w_pool": nrm((n, POOL_GROUPS, POOL_HD, POOL_HD), POOL_HD ** -0.5),
        "s_pool": gain((n, GROUP_W)),
        "w_out": nrm((n, MIX_W, D_MODEL), MIX_W ** -0.5),
        "g_ffn2": gain((n, D_MODEL)),
        "w13_ffn2": nrm((n, D_MODEL, 2 * D_FF), D_MODEL ** -0.5),
        "w2_ffn2": nrm((n, D_FF, D_MODEL), D_FF ** -0.5),
        "g_final": gain((D_MODEL,)),
    }


def reference(x, c, ctx, c_ctx, w_ada, b_ada, g_ffn1, w13_ffn1, w2_ffn1, g_mix, w_in, w_fnet,
              g_q, w_uq, g_kv, w_ukv, g_sgu, w_sgu, b_sgu, w_pool, s_pool, w_out,
              g_ffn2, w13_ffn2, w2_ffn2, g_final):
    B, L, _ = x.shape
    rope = axial_rope_tables(L)
    silu_c = jax.nn.silu(c)
    silu_cc = jax.nn.silu(c_ctx)[None]
    h, hc = x, ctx
    for i in range(DEPTH):
        last = i == DEPTH - 1
        m = jnp.split((silu_c @ w_ada[i] + b_ada[i])[:, None, :], N_MOD, axis=-1)
        mc = jnp.split((silu_cc @ w_ada[i] + b_ada[i])[:, None, :], N_MOD, axis=-1)

        h = ffn_half_step(h, g_ffn1[i], m[0], m[1], m[2], w13_ffn1[i], w2_ffn1[i])
        hc = ffn_half_step(hc, g_ffn1[i], mc[0], mc[1], mc[2], w13_ffn1[i], w2_ffn1[i])

        n_lat = modulate(rms_norm(h, g_mix[i]), m[3], m[4])
        n_ctx = modulate(rms_norm(hc, g_mix[i]), mc[3], mc[4])
        z = n_lat @ w_in[i]
        zc = n_ctx @ (w_in[i][:, OFF_KV:OFF_G] if last else w_in[i])
        zc_kv = zc if last else zc[..., OFF_KV:OFF_G]
        kc, vc = mla_keys_values(zc_kv[..., :KV_RANK], zc_kv[..., KV_RANK:], g_kv[i], w_ukv[i], None)
        k, v = mla_keys_values(z[..., OFF_KV:OFF_KR], z[..., OFF_KR:OFF_G], g_kv[i], w_ukv[i], rope)
        mix_p = (w_fnet[i], g_q[i], w_uq[i], g_sgu[i], w_sgu[i], b_sgu[i], w_pool[i], s_pool[i])
        y = mix_heads(z, jnp.concatenate([k, kc], axis=1), jnp.concatenate([v, vc], axis=1), rope, *mix_p) @ w_out[i]
        h = h + m[5] * y

        h = ffn_half_step(h, g_ffn2[i], m[6], m[7], m[8], w13_ffn2[i], w2_ffn2[i])

        if not last:
            yc = mix_heads(zc, kc, vc, None, *mix_p) @ w_out[i]
            hc = hc + mc[5] * yc
            hc = ffn_half_step(hc, g_ffn2[i], mc[6], mc[7], mc[8], w13_ffn2[i], w2_ffn2[i])
    return rms_norm(h, g_final)
```

```python
import functools
import math

import numpy as np
import jax
import jax.numpy as jnp
from jax import lax
from jax.experimental import pallas as pl
from jax.experimental.pallas import tpu as pltpu

F32 = jnp.float32
BF16 = jnp.bfloat16

D_MODEL = 1024
DEPTH = 2
GRID_W = 64
EPS = 1e-6
N_MOD = 9
GROUP_W = 256
HEADS = 4
HEAD_W = GROUP_W // HEADS
QK_NOPE = 64
QK_ROPE = 32
AXIS_ROPE = QK_ROPE // 2
Q_RANK = 192
KV_RANK = 128
ROPE_BASE = 10000.0
SGU_CHUNK = 128
POOL_WINDOWS = (2, 4, 8, 16)
D_FF = 2816
HEAD_PAD = 128
V_PAD = 96
KEY_CHUNK = 256
SCORE_LOOKAHEAD = 8
FLIP = 256
ROPE_LANE0 = QK_NOPE

OFF_F = 0
OFF_Q = OFF_F + GROUP_W
OFF_KV = OFF_Q + Q_RANK
OFF_KR = OFF_KV + KV_RANK
OFF_G = OFF_KR + QK_ROPE
OFF_P = OFF_G + 2 * GROUP_W
IN_W = OFF_P + GROUP_W

MAX_HALF_WINDOW = max(POOL_WINDOWS) // 2
POOL_PAD = 16
VMEM_LIMIT = 56 * 1024 * 1024


def _cparams(*sem):
    return pltpu.CompilerParams(dimension_semantics=sem, vmem_limit_bytes=VMEM_LIMIT)


def _const_spec(shape):
    nd = len(shape)
    return pl.BlockSpec(shape, lambda *_: (0,) * nd, pipeline_mode=pl.Buffered(1))


def _dot(a, b):
    return jnp.dot(a, b, preferred_element_type=F32)


def _rms(x):
    return x * lax.rsqrt(jnp.mean(x * x, axis=-1, keepdims=True) + EPS)


def _norm_mod(x, g, shift, scale):
    return (_rms(x) * g) * (1.0 + scale) + shift


def _ada_kernel(c_ref, w_ref, b_ref, o_ref):
    c = c_ref[...]
    s = c * jax.nn.sigmoid(c)
    o_ref[0] = jnp.dot(s, w_ref[0], preferred_element_type=F32,
                       precision=lax.Precision.HIGHEST) + b_ref[0]


def _ada(cond, w_ada, b_ada):
    rows, d = cond.shape
    depth, _, width = w_ada.shape
    bn = 1152
    return pl.pallas_call(
        _ada_kernel,
        out_shape=jax.ShapeDtypeStruct((depth, rows, width), F32),
        grid=(depth, width // bn),
        in_specs=[
            pl.BlockSpec((rows, d), lambda i, j: (0, 0)),
            pl.BlockSpec((1, d, bn), lambda i, j: (i, 0, j)),
            pl.BlockSpec((1, 1, bn), lambda i, j: (i, 0, j)),
        ],
        out_specs=pl.BlockSpec((1, rows, bn), lambda i, j: (i, 0, j)),
        compiler_params=_cparams("parallel", "parallel"),
        name="ada",
    )(cond, w_ada, b_ada.reshape(depth, 1, width))


def _ffn_kernel(h_ref, mod_ref, g_ref, w1_ref, w3_ref, w2_ref, *rest, row0, fchunk, final_norm):
    if final_norm:
        gf_ref, o_ref, xn_ref, act_ref = rest
    else:
        o_ref, xn_ref, act_ref = rest
    x = h_ref[0]
    shift = mod_ref[0, row0:row0 + 1, :]
    scale = mod_ref[0, row0 + 1:row0 + 2, :]
    gate = mod_ref[0, row0 + 2:row0 + 3, :]
    xn_ref[...] = _norm_mod(x, g_ref[...], shift, scale).astype(BF16)
    for j in range(D_FF // fchunk):
        sl = slice(j * fchunk, (j + 1) * fchunk)
        a = _dot(xn_ref[...], w1_ref[:, sl])
        b = _dot(xn_ref[...], w3_ref[:, sl])
        act_ref[:, sl] = (a * jax.nn.sigmoid(a) * b).astype(BF16)
    y = _dot(act_ref[...], w2_ref[...])
    out = x + (0.5 * gate) * y
    if final_norm:
        out = _rms(out) * gf_ref[...]
    o_ref[0] = out


def _ffn(h, mods, g, w1, w3, w2, *, row0, tm, g_final=None):
    b, l, d = h.shape
    per_batch = mods.shape[0] > 1
    final_norm = g_final is not None
    kern = functools.partial(_ffn_kernel, row0=row0, fchunk=256, final_norm=final_norm)
    extra = [g_final] if final_norm else []
    return pl.pallas_call(
        kern,
        out_shape=jax.ShapeDtypeStruct(h.shape, F32),
        grid=(b, l // tm),
        in_specs=[
            pl.BlockSpec((1, tm, d), lambda i, j: (i, j, 0)),
            pl.BlockSpec((1, N_MOD, d), (lambda i, j: (i, 0, 0)) if per_batch else (lambda i, j: (0, 0, 0))),
            _const_spec((1, d)),
            _const_spec((d, D_FF)),
            _const_spec((d, D_FF)),
            _const_spec((D_FF, d)),
        ] + [_const_spec((1, d))] * len(extra),
        out_specs=pl.BlockSpec((1, tm, d), lambda i, j: (i, j, 0)),
        scratch_shapes=[pltpu.VMEM((tm, d), BF16), pltpu.VMEM((tm, D_FF), BF16)],
        compiler_params=_cparams("parallel", "parallel"),
        name="ffn",
    )(h, mods, g, w1, w3, w2, *extra)


def _gelu_tanh(x):
    return 0.5 * x * (1.0 + jnp.tanh(0.7978845608028654 * (x + 0.044715 * (x * x * x))))


def _group_mean(x, ones_bd):
    hi = x.astype(BF16)
    lo = (x - hi.astype(F32)).astype(BF16)
    return _dot(hi, ones_bd) + _dot(lo, ones_bd)


def _inproj_kernel(h_ref, mod_ref, g_ref, cos_ref, sin_ref,
                   wf_ref, dftc_ref, wq_ref, gq_ref, wuq_ref, wuqs_ref,
                   wkv_ref, gkv_ref, wkr_ref, wkrs_ref, wuk_ref, wuv_ref, vone_ref,
                   wg_ref, gsgu_ref, ones_ref, wsgu_ref, bsgu_ref, wp_ref,
                   *out_refs, kv_only, tm, q_scale):
    if kv_only:
        k_ref, vt_ref = out_refs
    else:
        zab_ref, qt_ref, k_ref, vt_ref, sgu_ref, zp_ref = out_refs
    x = h_ref[0]
    shift = mod_ref[0, 3:4, :]
    scale = mod_ref[0, 4:5, :]
    n = _norm_mod(x, g_ref[...], shift, scale).astype(BF16)
    cos = cos_ref[...]
    sin = sin_ref[...]

    kvn = (_rms(_dot(n, wkv_ref[...])) * gkv_ref[...]).astype(BF16)
    kr = _dot(n, wkr_ref[...]) * cos + _dot(n, wkrs_ref[...]) * sin
    k = _dot(kvn, wuk_ref[...]) + jnp.concatenate([kr] * HEADS, axis=1)
    k_ref[0] = k.astype(BF16)
    vt_ref[0] = (_dot(kvn, wuv_ref[...]) + vone_ref[...]).T.astype(BF16)
    if kv_only:
        return

    zf = _dot(n, wf_ref[...]).astype(BF16)
    zab_ref[0] = _dot(zf, dftc_ref[...]).astype(BF16)

    qn = (_rms(_dot(n, wq_ref[...])) * gq_ref[...]).astype(BF16)
    cos4 = jnp.concatenate([cos] * HEADS, axis=1)
    sin4 = jnp.concatenate([sin] * HEADS, axis=1)
    q = _dot(qn, wuq_ref[...]) * cos4 + _dot(qn, wuqs_ref[...]) * sin4
    qt_ref[0] = (q * q_scale).T.astype(BF16)

    gz = _gelu_tanh(_dot(n, wg_ref[...]))
    u = gz[:, :GROUP_W]
    vv = gz[:, GROUP_W:]
    ms = _group_mean(vv * vv, ones_ref[...])
    vn = (vv * lax.rsqrt(ms + EPS) * gsgu_ref[...]).astype(BF16)
    lane_head = lax.broadcasted_iota(jnp.int32, (SGU_CHUNK, GROUP_W), 1) // HEAD_W
    for c in range(tm // SGU_CHUNK):
        rows = slice(c * SGU_CHUNK, (c + 1) * SGU_CHUNK)
        r = _dot(wsgu_ref[...], vn[rows])
        sel = r[0:SGU_CHUNK]
        for hd in range(1, HEADS):
            sel = jnp.where(lane_head == hd, r[hd * SGU_CHUNK:(hd + 1) * SGU_CHUNK], sel)
        sgu_ref[0, rows, :] = (u[rows] * (sel + bsgu_ref[...])).astype(BF16)

    zp_ref[0] = _dot(n, wp_ref[...])


def _inproj(h, mods, cos, sin, p, *, kv_only, tm):
    b, l, d = h.shape
    per_batch = mods.shape[0] > 1
    weights = [p["w_f"], p["dft_c"], p["w_q"], p["g_q"], p["w_uq"], p["w_uq_swap"],
               p["w_kv"], p["g_kv"], p["w_kr"], p["w_kr_swap"], p["w_uk"], p["w_uv"], p["v_one"],
               p["w_g"], p["g_sgu"], p["ones_bd"], p["w_sgu"], p["b_sgu"], p["w_p"]]
    tok = lambda w, dt: jax.ShapeDtypeStruct((b, l, w), dt)
    tok_spec = lambda w: pl.BlockSpec((1, tm, w), lambda i, j: (i, j, 0))
    tr = lambda w: jax.ShapeDtypeStruct((b, w, l), BF16)
    tr_spec = lambda w: pl.BlockSpec((1, w, tm), lambda i, j: (i, 0, j))
    if kv_only:
        out_shape = [tok(HEADS * HEAD_PAD, BF16), tr(HEADS * V_PAD)]
        out_specs = [tok_spec(HEADS * HEAD_PAD), tr_spec(HEADS * V_PAD)]
    else:
        out_shape = [tok(2 * GROUP_W, BF16), tr(HEADS * HEAD_PAD), tok(HEADS * HEAD_PAD, BF16),
                     tr(HEADS * V_PAD), tok(GROUP_W, BF16), tok(GROUP_W, F32)]
        out_specs = [tok_spec(2 * GROUP_W), tr_spec(HEADS * HEAD_PAD), tok_spec(HEADS * HEAD_PAD),
                     tr_spec(HEADS * V_PAD), tok_spec(GROUP_W), tok_spec(GROUP_W)]
    kern = functools.partial(_inproj_kernel, kv_only=kv_only, tm=tm,
                             q_scale=float(QK_NOPE + QK_ROPE) ** -0.5 * math.log2(math.e))
    return pl.pallas_call(
        kern,
        out_shape=out_shape,
        grid=(b, l // tm),
        in_specs=[
            pl.BlockSpec((1, tm, d), lambda i, j: (i, j, 0)),
            pl.BlockSpec((1, N_MOD, d), (lambda i, j: (i, 0, 0)) if per_batch else (lambda i, j: (0, 0, 0))),
            _const_spec((1, d)),
            pl.BlockSpec((tm, HEAD_PAD), lambda i, j: (j, 0)),
            pl.BlockSpec((tm, HEAD_PAD), lambda i, j: (j, 0)),
        ] + [_const_spec(w.shape) for w in weights],
        out_specs=out_specs,
        compiler_params=_cparams("parallel", "parallel"),
        name="inproj",
    )(h, mods, p["g_mix"], cos, sin, *weights)


def _dft_kernel(c_ref, s_ref, z_ref, o_ref):
    za = z_ref[0, :, :GROUP_W]
    zb = z_ref[0, :, GROUP_W:]
    o_ref[0] = (_dot(c_ref[...], za) - _dot(s_ref[...], zb)).astype(BF16)


def _dft(zab, cmat, smat, *, tm):
    b, l, _ = zab.shape
    return pl.pallas_call(
        _dft_kernel,
        out_shape=jax.ShapeDtypeStruct((b, l, GROUP_W), BF16),
        grid=(l // tm, b),
        in_specs=[
            pl.BlockSpec((tm, l), lambda i, j: (i, 0)),
            pl.BlockSpec((tm, l), lambda i, j: (i, 0)),
            pl.BlockSpec((1, l, 2 * GROUP_W), lambda i, j: (j, 0, 0)),
        ],
        out_specs=pl.BlockSpec((1, tm, GROUP_W), lambda i, j: (j, i, 0)),
        compiler_params=_cparams("parallel", "parallel"),
        name="dft",
    )(cmat, smat, zab)


def _dft_fold_kernel(cq_ref, sq_ref, rev_ref, z_ref, o_ref, fold_ref, t_ref, *, l):
    half = l // 2
    nb = half // FLIP
    rev = rev_ref[...]
    fold_ref[0:8, :] = jnp.zeros((8, 2 * GROUP_W), F32)
    for jb in range(nb):
        blk = z_ref[0, (2 * nb - 1 - jb) * FLIP:(2 * nb - jb) * FLIP, :]
        fold_ref[8 + jb * FLIP:8 + (jb + 1) * FLIP, :] = _dot(rev, blk)
    zlo = z_ref[0, 0:half, :].astype(F32)
    mirrored = fold_ref[7:7 + half, :]
    even = (zlo[:, :GROUP_W] + mirrored[:, :GROUP_W]).astype(BF16)
    odd = (zlo[:, GROUP_W:] - mirrored[:, GROUP_W:]).astype(BF16)
    p = _dot(cq_ref[...], even)
    q = _dot(sq_ref[...], odd)
    k_idx = lax.broadcasted_iota(jnp.int32, (p.shape[0], 1), 0)
    sign = (1 - 2 * (k_idx & 1)).astype(F32) * (1.0 / math.sqrt(l))
    p = p + sign * z_ref[0, half:half + 1, :GROUP_W].astype(F32)
    o_ref[0, 0:half, :] = (p[:half] - q[:half]).astype(BF16)
    t_ref[...] = p + q
    upper = t_ref[1:half + 1, :].astype(BF16)
    for jb in range(nb):
        blk = upper[(nb - 1 - jb) * FLIP:(nb - jb) * FLIP, :]
        o_ref[0, half + jb * FLIP:half + (jb + 1) * FLIP, :] = _dot(rev, blk).astype(BF16)


def _dft_fold(zab, cq, sq, rev):
    b, l, _ = zab.shape
    half = l // 2
    rows = cq.shape[0]
    return pl.pallas_call(
        functools.partial(_dft_fold_kernel, l=l),
        out_shape=jax.ShapeDtypeStruct((b, l, GROUP_W), BF16),
        grid=(b,),
        in_specs=[
            _const_spec((rows, half)),
            _const_spec((rows, half)),
            _const_spec((FLIP, FLIP)),
            pl.BlockSpec((1, l, 2 * GROUP_W), lambda i: (i, 0, 0)),
        ],
        out_specs=pl.BlockSpec((1, l, GROUP_W), lambda i: (i, 0, 0)),
        scratch_shapes=[pltpu.VMEM((half + 8, 2 * GROUP_W), F32), pltpu.VMEM((rows, GROUP_W), F32)],
        compiler_params=_cparams("parallel"),
        name="dft_fold",
    )(cq, sq, rev, zab)


def _attn_kernel(qt_ref, *refs, seg_lens):
    n_seg = len(seg_lens)
    kv_refs = refs[:2 * n_seg]
    o_ref = refs[2 * n_seg]
    acc_ref = refs[2 * n_seg + 1]
    items = []
    for s, lk in enumerate(seg_lens):
        chunk = min(KEY_CHUNK, lk)
        for c0 in range(0, lk, chunk):
            items += [(s, c0, chunk, hd) for hd in range(HEADS)]
    m = [None] * HEADS
    acc = [None] * HEADS
    scores = {}
    for i in range(len(items) + SCORE_LOOKAHEAD):
        if i < len(items):
            s, c0, chunk, hd = items[i]
            k = kv_refs[2 * s][0, c0:c0 + chunk, hd * HEAD_PAD:(hd + 1) * HEAD_PAD]
            scores[i] = _dot(k, qt_ref[0, hd * HEAD_PAD:(hd + 1) * HEAD_PAD, :])
        j = i - SCORE_LOOKAHEAD
        if j < 0:
            continue
        s, c0, chunk, hd = items[j]
        sc = scores.pop(j)
        cm = sc.max(axis=0, keepdims=True)
        m_new = cm if m[hd] is None else jnp.maximum(m[hd], cm)
        p = jnp.exp2(sc - m_new).astype(BF16)
        vt = kv_refs[2 * s + 1][0, hd * V_PAD:(hd + 1) * V_PAD, c0:c0 + chunk]
        pv = _dot(vt, p)
        acc[hd] = pv if acc[hd] is None else acc[hd] * jnp.exp2(m[hd] - m_new) + pv
        m[hd] = m_new
    for hd in range(HEADS):
        inv = 1.0 / acc[hd][HEAD_W:HEAD_W + 1]
        acc_ref[hd * HEAD_W:(hd + 1) * HEAD_W, :] = acc[hd][:HEAD_W] * inv
    o_ref[0] = acc_ref[...].T.astype(BF16)


def _attention(qt, segments, *, tq):
    b, _, l = qt.shape
    in_specs = [pl.BlockSpec((1, HEADS * HEAD_PAD, tq), lambda i, j: (i, 0, j))]
    args = [qt]
    for k, vt in segments:
        lk = k.shape[1]
        in_specs.append(pl.BlockSpec((1, lk, HEADS * HEAD_PAD), lambda i, j: (i, 0, 0)))
        in_specs.append(pl.BlockSpec((1, HEADS * V_PAD, lk), lambda i, j: (i, 0, 0)))
        args += [k, vt]
    return pl.pallas_call(
        functools.partial(_attn_kernel, seg_lens=tuple(k.shape[1] for k, _ in segments)),
        out_shape=jax.ShapeDtypeStruct((b, l, GROUP_W), BF16),
        grid=(b, l // tq),
        in_specs=in_specs,
        out_specs=pl.BlockSpec((1, tq, GROUP_W), lambda i, j: (i, j, 0)),
        scratch_shapes=[pltpu.VMEM((GROUP_W, tq), F32)],
        compiler_params=_cparams("parallel", "parallel"),
        name="attention",
    )(*args)


def _pool_kernel(z_ref, wp_ref, sp_ref, o_ref, pad_ref, *, l, rb):
    zeros = jnp.zeros((POOL_PAD, GROUP_W), F32)
    pad_ref[0:POOL_PAD, :] = zeros
    pad_ref[POOL_PAD + l:2 * POOL_PAD + l, :] = zeros
    pad_ref[POOL_PAD:POOL_PAD + l, :] = z_ref[0]
    group = lax.broadcasted_iota(jnp.int32, (rb, GROUP_W), 1) // HEAD_W
    half = jnp.full((rb, GROUP_W), POOL_WINDOWS[0] // 2, jnp.int32)
    for gi in range(1, len(POOL_WINDOWS)):
        half = jnp.where(group == gi, POOL_WINDOWS[gi] // 2, half)
    for r in range(l // rb):
        base = POOL_PAD + r * rb

        def shifted(kk):
            return pad_ref[base + kk:base + kk + rb, :]

        z = shifted(0)
        win = shifted(-1) + z
        sel = win
        lo, hi = -1, 1
        for gi in range(1, len(POOL_WINDOWS)):
            hw = POOL_WINDOWS[gi] // 2
            for kk in list(range(-hw, lo)) + list(range(hi, hw)):
                win = win + shifted(kk)
            lo, hi = -hw, hw
            sel = jnp.where(group == gi, win, sel)
        t = lax.broadcasted_iota(jnp.int32, (rb, GROUP_W), 0) + r * rb
        cnt = jnp.minimum(t + half, l) - jnp.maximum(t - half, 0)
        pooled = sel / cnt.astype(F32) - z
        y = _dot(pooled.astype(BF16), wp_ref[...]) * sp_ref[...]
        o_ref[0, r * rb:(r + 1) * rb, :] = y.astype(BF16)


def _pool(zp, wp_bd, s_pool):
    b, l, _ = zp.shape
    rb = min(l, 256)
    return pl.pallas_call(
        functools.partial(_pool_kernel, l=l, rb=rb),
        out_shape=jax.ShapeDtypeStruct((b, l, GROUP_W), BF16),
        grid=(b,),
        in_specs=[
            pl.BlockSpec((1, l, GROUP_W), lambda i: (i, 0, 0)),
            _const_spec((GROUP_W, GROUP_W)),
            _const_spec((1, GROUP_W)),
        ],
        out_specs=pl.BlockSpec((1, l, GROUP_W), lambda i: (i, 0, 0)),
        scratch_shapes=[pltpu.VMEM((l + 2 * POOL_PAD, GROUP_W), F32)],
        compiler_params=_cparams("parallel"),
        name="pool",
    )(zp, wp_bd, s_pool)


def _outproj_kernel(h_ref, mod_ref, f_ref, att_ref, sgu_ref, pool_ref, wf_ref, wo_ref, o_ref):
    fmix = _dot(f_ref[0], wf_ref[...]).astype(BF16)
    mixed = jnp.concatenate([fmix, att_ref[0], sgu_ref[0], pool_ref[0]], axis=1)
    y = _dot(mixed, wo_ref[...])
    o_ref[0] = h_ref[0] + mod_ref[0, 5:6, :] * y


def _outproj(h, mods, fre, att, sgu, pool, wf_bd, w_out, *, tm):
    b, l, d = h.shape
    per_batch = mods.shape[0] > 1
    grp = pl.BlockSpec((1, tm, GROUP_W), lambda i, j: (i, j, 0))
    return pl.pallas_call(
        _outproj_kernel,
        out_shape=jax.ShapeDtypeStruct(h.shape, F32),
        grid=(b, l // tm),
        in_specs=[
            pl.BlockSpec((1, tm, d), lambda i, j: (i, j, 0)),
            pl.BlockSpec((1, N_MOD, d), (lambda i, j: (i, 0, 0)) if per_batch else (lambda i, j: (0, 0, 0))),
            grp, grp, grp, grp,
            _const_spec((GROUP_W, GROUP_W)),
            _const_spec((4 * GROUP_W, d)),
        ],
        out_specs=pl.BlockSpec((1, tm, d), lambda i, j: (i, j, 0)),
        compiler_params=_cparams("parallel", "parallel"),
        name="outproj",
    )(h, mods, fre, att, sgu, pool, wf_bd, w_out)


def _block_diag(blocks):
    n, r, c = blocks.shape
    out = jnp.zeros((n * r, n * c), blocks.dtype)
    for i in range(n):
        out = lax.dynamic_update_slice(out, blocks[i], (i * r, i * c))
    return out


def _channel_dft():
    k = np.arange(HEAD_W)
    ang = 2.0 * np.pi * np.outer(k, k) / HEAD_W
    eye = np.eye(HEADS)
    c = np.kron(eye, np.cos(ang)) / math.sqrt(HEAD_W)
    s = np.kron(eye, np.sin(ang)) / math.sqrt(HEAD_W)
    return jnp.asarray(np.concatenate([c, s], axis=1), F32).astype(BF16)


def _token_dft(l, rows=None, cols=None):
    kk = jnp.arange(l if rows is None else rows, dtype=jnp.int32)
    tt = jnp.arange(l if cols is None else cols, dtype=jnp.int32)
    ang = ((kk[:, None] * tt[None, :]) % l).astype(F32) * (2.0 * math.pi / l)
    norm = 1.0 / math.sqrt(l)
    return (jnp.cos(ang) * norm).astype(BF16), (jnp.sin(ang) * norm).astype(BF16)


def _folded_token_dft(l):
    half = l // 2
    cq, sq = _token_dft(l, rows=half + 16, cols=half)
    rev = jnp.asarray(np.eye(FLIP)[::-1].copy(), F32).astype(BF16)
    return cq, sq, rev


def _rope_partner():
    d = np.arange(QK_ROPE)
    first = (d % AXIS_ROPE) < AXIS_ROPE // 2
    partner = np.where(first, d + AXIS_ROPE // 2, d - AXIS_ROPE // 2)
    sign = np.where(first, -1.0, 1.0).astype(np.float32)
    return partner, sign


def _rope_tables(l):
    pos = jnp.arange(l)
    row = (pos // GRID_W).astype(F32)
    col = (pos % GRID_W).astype(F32)
    inv = jnp.power(ROPE_BASE, -jnp.arange(0, AXIS_ROPE, 2, dtype=F32) / AXIS_ROPE)
    ang = jnp.concatenate([row[:, None] * inv, row[:, None] * inv,
                           col[:, None] * inv, col[:, None] * inv], axis=1)
    cos = jnp.ones((l, HEAD_PAD), F32).at[:, ROPE_LANE0:ROPE_LANE0 + QK_ROPE].set(jnp.cos(ang))
    sin = jnp.zeros((l, HEAD_PAD), F32).at[:, ROPE_LANE0:ROPE_LANE0 + QK_ROPE].set(jnp.sin(ang))
    return cos, sin


def _identity_rope_tables(l):
    return jnp.ones((l, HEAD_PAD), F32), jnp.zeros((l, HEAD_PAD), F32)


def _swap_cols(w):
    partner, sign = _rope_partner()
    return w[..., partner] * sign


def _layer_params(i, w13_ffn1, w2_ffn1, g_ffn1, g_mix, w_in, w_fnet, g_q, w_uq, g_kv, w_ukv,
                  g_sgu, w_sgu, b_sgu, w_pool, s_pool, w_out, g_ffn2, w13_ffn2, w2_ffn2):
    p = {}
    for tag, w13, w2, g in (("1", w13_ffn1, w2_ffn1, g_ffn1), ("2", w13_ffn2, w2_ffn2, g_ffn2)):
        p["w1_" + tag] = w13[i][:, :D_FF].astype(BF16)
        p["w3_" + tag] = w13[i][:, D_FF:].astype(BF16)
        p["w2_" + tag] = w2[i].astype(BF16)
        p["g_" + tag] = g[i].reshape(1, D_MODEL)
    wi = w_in[i]
    p["g_mix"] = g_mix[i].reshape(1, D_MODEL)
    p["w_f"] = wi[:, OFF_F:OFF_Q].astype(BF16)
    p["w_q"] = wi[:, OFF_Q:OFF_KV].astype(BF16)
    p["w_kv"] = wi[:, OFF_KV:OFF_KR].astype(BF16)
    wkr = wi[:, OFF_KR:OFF_G]
    pad_kr = lambda w: jnp.zeros((D_MODEL, HEAD_PAD), F32).at[:, ROPE_LANE0:ROPE_LANE0 + QK_ROPE].set(w).astype(BF16)
    p["w_kr"] = pad_kr(wkr)
    p["w_kr_swap"] = pad_kr(_swap_cols(wkr))
    p["w_g"] = wi[:, OFF_G:OFF_P].astype(BF16)
    p["w_p"] = wi[:, OFF_P:].astype(BF16)
    p["dft_c"] = _channel_dft()
    p["g_q"] = g_q[i].reshape(1, Q_RANK)
    wuq = w_uq[i].reshape(Q_RANK, HEADS, QK_NOPE + QK_ROPE)
    nope, rope = wuq[..., :QK_NOPE], wuq[..., QK_NOPE:]
    zpad = jnp.zeros((Q_RANK, HEADS, HEAD_PAD - QK_NOPE - QK_ROPE), F32)
    p["w_uq"] = jnp.concatenate([nope, rope, zpad], axis=-1).reshape(Q_RANK, HEADS * HEAD_PAD).astype(BF16)
    p["w_uq_swap"] = jnp.concatenate([jnp.zeros_like(nope), _swap_cols(rope), zpad], axis=-1).reshape(
        Q_RANK, HEADS * HEAD_PAD).astype(BF16)
    p["g_kv"] = g_kv[i].reshape(1, KV_RANK)
    wukv = w_ukv[i].reshape(KV_RANK, HEADS, QK_NOPE + HEAD_W)
    p["w_uk"] = jnp.concatenate([wukv[..., :QK_NOPE], jnp.zeros((KV_RANK, HEADS, HEAD_PAD - QK_NOPE), F32)],
                                axis=-1).reshape(KV_RANK, HEADS * HEAD_PAD).astype(BF16)
    p["w_uv"] = jnp.concatenate([wukv[..., QK_NOPE:], jnp.zeros((KV_RANK, HEADS, V_PAD - HEAD_W), F32)],
                                axis=-1).reshape(KV_RANK, HEADS * V_PAD).astype(BF16)
    p["v_one"] = jnp.zeros((1, HEADS, V_PAD), F32).at[:, :, HEAD_W].set(1.0).reshape(1, HEADS * V_PAD)
    p["g_sgu"] = g_sgu[i].reshape(1, GROUP_W)
    p["ones_bd"] = jnp.asarray(np.kron(np.eye(HEADS), np.full((HEAD_W, HEAD_W), 1.0 / HEAD_W)), BF16)
    p["w_sgu"] = w_sgu[i].reshape(HEADS * SGU_CHUNK, SGU_CHUNK).astype(BF16)
    p["b_sgu"] = jnp.repeat(b_sgu[i].T, HEAD_W, axis=1)
    p["wf_bd"] = _block_diag(w_fnet[i]).astype(BF16)
    p["wp_bd"] = _block_diag(w_pool[i]).astype(BF16)
    p["s_pool"] = s_pool[i].reshape(1, GROUP_W)
    p["w_out"] = w_out[i].astype(BF16)
    return p


def _mix(h, mods, rope, dft, kv_extra, p, *, tm, tq):
    zab, q, k, v, sgu, zp = _inproj(h, mods, rope[0], rope[1], p, kv_only=False, tm=tm)
    if len(dft) == 3:
        fre = _dft_fold(zab, *dft)
    else:
        fre = _dft(zab, dft[0], dft[1], tm=min(512, h.shape[1]))
    att = _attention(q, [(k, v)] + kv_extra, tq=tq)
    pool = _pool(zp, p["wp_bd"], p["s_pool"])
    return _outproj(h, mods, fre, att, sgu, pool, p["wf_bd"], p["w_out"], tm=tm), (k, v)


def kernel(x, c, ctx, c_ctx, w_ada, b_ada, g_ffn1, w13_ffn1, w2_ffn1, g_mix, w_in, w_fnet, g_q, w_uq, g_kv, w_ukv, g_sgu, w_sgu, b_sgu, w_pool, s_pool, w_out, g_ffn2, w13_ffn2, w2_ffn2, g_final):
    b, l, d = x.shape
    lc = ctx.shape[1]
    depth = w_ada.shape[0]
    tm = 512
    tmc = min(lc, 256)

    cond_rows = 16
    cond = jnp.zeros((cond_rows, d), F32).at[:b].set(c).at[b].set(c_ctx)
    mods_all = _ada(cond, w_ada, b_ada)

    rope = _rope_tables(l)
    rope_c = _identity_rope_tables(lc)
    dft = _folded_token_dft(l)
    dft_c = _token_dft(lc)

    h, hc = x, ctx
    for i in range(depth):
        last = i == depth - 1
        p = _layer_params(i, w13_ffn1, w2_ffn1, g_ffn1, g_mix, w_in, w_fnet, g_q, w_uq, g_kv, w_ukv,
                          g_sgu, w_sgu, b_sgu, w_pool, s_pool, w_out, g_ffn2, w13_ffn2, w2_ffn2)
        m = mods_all[i, :b].reshape(b, N_MOD, d)
        mc = mods_all[i, b:b + 1].reshape(1, N_MOD, d)

        h = _ffn(h, m, p["g_1"], p["w1_1"], p["w3_1"], p["w2_1"], row0=0, tm=tm)
        hc = _ffn(hc, mc, p["g_1"], p["w1_1"], p["w3_1"], p["w2_1"], row0=0, tm=tmc)

        if last:
            kc, vc = _inproj(hc, mc, rope_c[0], rope_c[1], p, kv_only=True, tm=tmc)
        else:
            hc_mixed, (kc, vc) = _mix(hc, mc, rope_c, dft_c, [], p, tm=tmc, tq=tmc)
        h, _ = _mix(h, m, rope, dft, [(kc, vc)], p, tm=tm, tq=256)
        h = _ffn(h, m, p["g_2"], p["w1_2"], p["w3_2"], p["w2_2"], row0=6, tm=tm,
                 g_final=g_final.reshape(1, d) if last else None)
        if not last:
            hc = _ffn(hc_mixed, mc, p["g_2"], p["w1_2"], p["w3_2"], p["w2_2"], row0=6, tm=tmc)
    return h
```

```python
import functools
import math

import numpy as np
import jax
import jax.numpy as jnp
from jax import lax
from jax.experimental import pallas as pl
from jax.experimental.pallas import tpu as pltpu

F32 = jnp.float32
BF16 = jnp.bfloat16

D_MODEL = 1024
DEPTH = 2
GRID_W = 64
EPS = 1e-6
N_MOD = 9
GROUP_W = 256
HEADS = 4
HEAD_W = GROUP_W // HEADS
QK_NOPE = 64
QK_ROPE = 32
AXIS_ROPE = QK_ROPE // 2
Q_RANK = 192
KV_RANK = 128
ROPE_BASE = 10000.0
SGU_CHUNK = 128
POOL_WINDOWS = (2, 4, 8, 16)
D_FF = 2816
HEAD_PAD = 128
V_PAD = 96
KEY_CHUNK = 256
SCORE_LOOKAHEAD = 8
FLIP = 256
TABLE_ROWS = 128
ROPE_LANE0 = QK_NOPE

OFF_F = 0
OFF_Q = OFF_F + GROUP_W
OFF_KV = OFF_Q + Q_RANK
OFF_KR = OFF_KV + KV_RANK
OFF_G = OFF_KR + QK_ROPE
OFF_P = OFF_G + 2 * GROUP_W
IN_W = OFF_P + GROUP_W

CAT_F = 0
CAT_G = CAT_F + GROUP_W
CAT_P = CAT_G + 2 * GROUP_W
CAT_Q = CAT_P + GROUP_W
Q_PAD = 256
CAT_KV = CAT_Q + Q_PAD
CAT_KR = CAT_KV + KV_RANK
CAT_W = CAT_KR + HEAD_PAD
ROPE_PAIR = AXIS_ROPE // 2
TQ = 512
TQ_SUB = 256

MAX_HALF_WINDOW = max(POOL_WINDOWS) // 2
POOL_PAD = 16
VMEM_LIMIT = 56 * 1024 * 1024


def _cparams(*sem):
    return pltpu.CompilerParams(dimension_semantics=sem, vmem_limit_bytes=VMEM_LIMIT)


def _const_spec(shape):
    nd = len(shape)
    return pl.BlockSpec(shape, lambda *_: (0,) * nd, pipeline_mode=pl.Buffered(1))


def _dot(a, b):
    return jnp.dot(a, b, preferred_element_type=F32)


def _rms(x):
    return x * lax.rsqrt(jnp.mean(x * x, axis=-1, keepdims=True) + EPS)


def _norm_mod(x, g, shift, scale):
    return (_rms(x) * g) * (1.0 + scale) + shift


def _ada_kernel(c_ref, w_ref, b_ref, o_ref):
    c = c_ref[...]
    s = c * jax.nn.sigmoid(c)
    hi = s.astype(BF16)
    lo = (s - hi.astype(F32)).astype(BF16)
    r = _dot(jnp.concatenate([hi, lo], axis=0), w_ref[0].astype(BF16))
    rows = c.shape[0]
    o_ref[0] = r[:rows] + r[rows:] + b_ref[0]


def _ada(cond, w_ada, b_ada):
    rows, d = cond.shape
    depth, _, width = w_ada.shape
    bn = 1152
    return pl.pallas_call(
        _ada_kernel,
        out_shape=jax.ShapeDtypeStruct((depth, rows, width), F32),
        grid=(depth, width // bn),
        in_specs=[
            pl.BlockSpec((rows, d), lambda i, j: (0, 0)),
            pl.BlockSpec((1, d, bn), lambda i, j: (i, 0, j)),
            pl.BlockSpec((1, 1, bn), lambda i, j: (i, 0, j)),
        ],
        out_specs=pl.BlockSpec((1, rows, bn), lambda i, j: (i, 0, j)),
        compiler_params=_cparams("parallel", "parallel"),
        name="ada",
    )(cond, w_ada, b_ada.reshape(depth, 1, width))


def _ffn_kernel(h_ref, mod_ref, g_ref, w13_ref, w2_ref, *rest, row0, fchunk, mix_in, final_norm):
    rest = list(rest)
    mix_refs = [rest.pop(0) for _ in range(6)] if mix_in else None
    gf_ref = rest.pop(0) if final_norm else None
    o_ref, xn_ref, act_ref = rest
    x = h_ref[0]
    if mix_in:
        f_ref, att_ref, sgu_ref, pool_ref, wf_ref, wo_ref = mix_refs
        fmix = _dot(f_ref[0], wf_ref[...]).astype(BF16)
        mixed = jnp.concatenate([fmix, att_ref[0], sgu_ref[0], pool_ref[0]], axis=1)
        x = x + mod_ref[0, 5:6, :] * _dot(mixed, wo_ref[...])
    shift = mod_ref[0, row0:row0 + 1, :]
    scale = mod_ref[0, row0 + 1:row0 + 2, :]
    gate = mod_ref[0, row0 + 2:row0 + 3, :]
    xn_ref[...] = _norm_mod(x, g_ref[...], shift, scale).astype(BF16)
    for j in range(D_FF // fchunk):
        sl = slice(j * fchunk, (j + 1) * fchunk)
        a = _dot(xn_ref[...], w13_ref[:, sl])
        b = _dot(xn_ref[...], w13_ref[:, D_FF + j * fchunk:D_FF + (j + 1) * fchunk])
        act_ref[:, sl] = (a * jax.nn.sigmoid(a) * b).astype(BF16)
    y = _dot(act_ref[...], w2_ref[...])
    out = x + (0.5 * gate) * y
    if final_norm:
        out = _rms(out) * gf_ref[...]
    o_ref[0] = out


def _ffn(h, mods, g, w13, w2, *, row0, tm, mix=None, g_final=None):
    b, l, d = h.shape
    per_batch = mods.shape[0] > 1
    kern = functools.partial(_ffn_kernel, row0=row0, fchunk=256,
                             mix_in=mix is not None, final_norm=g_final is not None)
    extra, extra_specs = [], []
    if mix is not None:
        extra += list(mix)
        extra_specs += [pl.BlockSpec((1, tm, GROUP_W), lambda i, j: (i, j, 0))] * 4
        extra_specs += [_const_spec((GROUP_W, GROUP_W)), _const_spec((4 * GROUP_W, d))]
    if g_final is not None:
        extra.append(g_final)
        extra_specs.append(_const_spec((1, d)))
    return pl.pallas_call(
        kern,
        out_shape=jax.ShapeDtypeStruct(h.shape, F32),
        grid=(b, l // tm),
        in_specs=[
            pl.BlockSpec((1, tm, d), lambda i, j: (i, j, 0)),
            pl.BlockSpec((1, N_MOD, d), (lambda i, j: (i, 0, 0)) if per_batch else (lambda i, j: (0, 0, 0))),
            _const_spec((1, d)),
            _const_spec((d, 2 * D_FF)),
            _const_spec((D_FF, d)),
        ] + extra_specs,
        out_specs=pl.BlockSpec((1, tm, d), lambda i, j: (i, j, 0)),
        scratch_shapes=[pltpu.VMEM((tm, d), BF16), pltpu.VMEM((tm, D_FF), BF16)],
        compiler_params=_cparams("parallel", "parallel"),
        name="ffn",
    )(h, mods, g, w13, w2, *extra)


def _gelu_tanh(x):
    return 0.5 * x * (1.0 + jnp.tanh(0.7978845608028654 * (x + 0.044715 * (x * x * x))))


def _group_mean(x, ones_bd):
    hi = x.astype(BF16)
    lo = (x - hi.astype(F32)).astype(BF16)
    return _dot(hi, ones_bd) + _dot(lo, ones_bd)


def _rope(x, cos, sin_lo, sin_hi):
    n = x.shape[1]
    up = pltpu.roll(x, n - ROPE_PAIR, axis=1)
    down = pltpu.roll(x, ROPE_PAIR, axis=1)
    return x * cos + up * sin_lo + down * sin_hi


def _inproj_kernel(h_ref, mod_ref, g_ref, cos_ref, slo_ref, shi_ref,
                   wcat_ref, dftc_ref, gq_ref, wuq_ref, gkv_ref, wuk_ref, wuv_ref, vone_ref,
                   gsgu_ref, ones_ref, wsgu_ref, bsgu_ref,
                   *out_refs, kv_only, tm, q_scale):
    if kv_only:
        k_ref, vt_ref = out_refs
    else:
        zab_ref, qt_ref, k_ref, vt_ref, sgu_ref, zp_ref = out_refs
    x = h_ref[0]
    shift = mod_ref[0, 3:4, :]
    scale = mod_ref[0, 4:5, :]
    n = _norm_mod(x, g_ref[...], shift, scale).astype(BF16)
    rope = (cos_ref[...], slo_ref[...], shi_ref[...])

    if kv_only:
        z = _dot(n, wcat_ref[:, CAT_KV:])
        ckv, kr = z[:, :KV_RANK], z[:, KV_RANK:]
    else:
        z = _dot(n, wcat_ref[...])
        ckv, kr = z[:, CAT_KV:CAT_KR], z[:, CAT_KR:]

    kvn = (_rms(ckv) * gkv_ref[...]).astype(BF16)
    k = _dot(kvn, wuk_ref[...]) + jnp.concatenate([_rope(kr, *rope)] * HEADS, axis=1)
    k_ref[0] = k.astype(BF16)
    vt_ref[0] = (_dot(kvn, wuv_ref[...]) + vone_ref[...]).T.astype(BF16)
    if kv_only:
        return

    zab_ref[0] = _dot(z[:, CAT_F:CAT_G].astype(BF16), dftc_ref[...]).astype(BF16)

    cq = z[:, CAT_Q:CAT_KV]
    ms = jnp.sum(cq * cq, axis=-1, keepdims=True) * (1.0 / Q_RANK)
    qn = (cq * lax.rsqrt(ms + EPS) * gq_ref[...]).astype(BF16)
    rope4 = [jnp.concatenate([t] * HEADS, axis=1) for t in rope]
    q = _rope(_dot(qn, wuq_ref[...]), *rope4)
    qt_ref[0] = (q * q_scale).T.astype(BF16)

    gz = _gelu_tanh(z[:, CAT_G:CAT_P])
    u = gz[:, :GROUP_W]
    vv = gz[:, GROUP_W:]
    ms = _group_mean(vv * vv, ones_ref[...])
    vn = (vv * lax.rsqrt(ms + EPS) * gsgu_ref[...]).astype(BF16)
    lane_head = lax.broadcasted_iota(jnp.int32, (SGU_CHUNK, GROUP_W), 1) // HEAD_W
    for c in range(tm // SGU_CHUNK):
        rows = slice(c * SGU_CHUNK, (c + 1) * SGU_CHUNK)
        r = _dot(wsgu_ref[...], vn[rows])
        sel = r[0:SGU_CHUNK]
        for hd in range(1, HEADS):
            sel = jnp.where(lane_head == hd, r[hd * SGU_CHUNK:(hd + 1) * SGU_CHUNK], sel)
        sgu_ref[0, rows, :] = (u[rows] * (sel + bsgu_ref[...])).astype(BF16)

    zp_ref[0] = z[:, CAT_P:CAT_Q]


def _inproj(h, mods, rope, p, *, kv_only, tm):
    b, l, d = h.shape
    per_batch = mods.shape[0] > 1
    weights = [p["w_cat"], p["dft_c"], p["g_q"], p["w_uq"], p["g_kv"], p["w_uk"], p["w_uv"], p["v_one"],
               p["g_sgu"], p["ones_bd"], p["w_sgu"], p["b_sgu"]]
    rope_spec = pl.BlockSpec((tm, HEAD_PAD), lambda i, j: (j, 0))
    tok = lambda w, dt: jax.ShapeDtypeStruct((b, l, w), dt)
    tok_spec = lambda w: pl.BlockSpec((1, tm, w), lambda i, j: (i, j, 0))
    tr = lambda w: jax.ShapeDtypeStruct((b, w, l), BF16)
    tr_spec = lambda w: pl.BlockSpec((1, w, tm), lambda i, j: (i, 0, j))
    if kv_only:
        out_shape = [tok(HEADS * HEAD_PAD, BF16), tr(HEADS * V_PAD)]
        out_specs = [tok_spec(HEADS * HEAD_PAD), tr_spec(HEADS * V_PAD)]
    else:
        out_shape = [tok(2 * GROUP_W, BF16), tr(HEADS * HEAD_PAD), tok(HEADS * HEAD_PAD, BF16),
                     tr(HEADS * V_PAD), tok(GROUP_W, BF16), tok(GROUP_W, F32)]
        out_specs = [tok_spec(2 * GROUP_W), tr_spec(HEADS * HEAD_PAD), tok_spec(HEADS * HEAD_PAD),
                     tr_spec(HEADS * V_PAD), tok_spec(GROUP_W), tok_spec(GROUP_W)]
    kern = functools.partial(_inproj_kernel, kv_only=kv_only, tm=tm,
                             q_scale=float(QK_NOPE + QK_ROPE) ** -0.5 * math.log2(math.e))
    return pl.pallas_call(
        kern,
        out_shape=out_shape,
        grid=(b, l // tm),
        in_specs=[
            pl.BlockSpec((1, tm, d), lambda i, j: (i, j, 0)),
            pl.BlockSpec((1, N_MOD, d), (lambda i, j: (i, 0, 0)) if per_batch else (lambda i, j: (0, 0, 0))),
            _const_spec((1, d)),
            rope_spec, rope_spec, rope_spec,
        ] + [_const_spec(w.shape) for w in weights],
        out_specs=out_specs,
        compiler_params=_cparams("parallel", "parallel"),
        name="inproj",
    )(h, mods, p["g_mix"], *rope, *weights)


def _dft_kernel(c_ref, s_ref, z_ref, o_ref):
    za = z_ref[0, :, :GROUP_W]
    zb = z_ref[0, :, GROUP_W:]
    o_ref[0] = (_dot(c_ref[...], za) - _dot(s_ref[...], zb)).astype(BF16)


def _dft(zab, cmat, smat, *, tm):
    b, l, _ = zab.shape
    return pl.pallas_call(
        _dft_kernel,
        out_shape=jax.ShapeDtypeStruct((b, l, GROUP_W), BF16),
        grid=(l // tm, b),
        in_specs=[
            pl.BlockSpec((tm, l), lambda i, j: (i, 0)),
            pl.BlockSpec((tm, l), lambda i, j: (i, 0)),
            pl.BlockSpec((1, l, 2 * GROUP_W), lambda i, j: (j, 0, 0)),
        ],
        out_specs=pl.BlockSpec((1, tm, GROUP_W), lambda i, j: (j, i, 0)),
        compiler_params=_cparams("parallel", "parallel"),
        name="dft",
    )(cmat, smat, zab)


def _dft_fold_kernel(cq_ref, sq_ref, rev_ref, z_ref, o_ref, fold_ref, t_ref, *, l):
    half = l // 2
    nb = half // FLIP
    rev = rev_ref[...]
    fold_ref[0:8, :] = jnp.zeros((8, 2 * GROUP_W), F32)
    for jb in range(nb):
        blk = z_ref[0, (2 * nb - 1 - jb) * FLIP:(2 * nb - jb) * FLIP, :]
        fold_ref[8 + jb * FLIP:8 + (jb + 1) * FLIP, :] = _dot(rev, blk)
    zlo = z_ref[0, 0:half, :].astype(F32)
    mirrored = fold_ref[7:7 + half, :]
    even = (zlo[:, :GROUP_W] + mirrored[:, :GROUP_W]).astype(BF16)
    odd = (zlo[:, GROUP_W:] - mirrored[:, GROUP_W:]).astype(BF16)
    p = _dot(cq_ref[...], even)
    q = _dot(sq_ref[...], odd)
    k_idx = lax.broadcasted_iota(jnp.int32, (p.shape[0], 1), 0)
    sign = (1 - 2 * (k_idx & 1)).astype(F32) * (1.0 / math.sqrt(l))
    p = p + sign * z_ref[0, half:half + 1, :GROUP_W].astype(F32)
    o_ref[0, 0:half, :] = (p[:half] - q[:half]).astype(BF16)
    t_ref[...] = p + q
    upper = t_ref[1:half + 1, :].astype(BF16)
    for jb in range(nb):
        blk = upper[(nb - 1 - jb) * FLIP:(nb - jb) * FLIP, :]
        o_ref[0, half + jb * FLIP:half + (jb + 1) * FLIP, :] = _dot(rev, blk).astype(BF16)


def _dft_fold(zab, cq, sq, rev):
    b, l, _ = zab.shape
    half = l // 2
    rows = half + 16
    return pl.pallas_call(
        functools.partial(_dft_fold_kernel, l=l),
        out_shape=jax.ShapeDtypeStruct((b, l, GROUP_W), BF16),
        grid=(b,),
        in_specs=[
            _const_spec((rows, half)),
            _const_spec((rows, half)),
            _const_spec((FLIP, FLIP)),
            pl.BlockSpec((1, l, 2 * GROUP_W), lambda i: (i, 0, 0)),
        ],
        out_specs=pl.BlockSpec((1, l, GROUP_W), lambda i: (i, 0, 0)),
        scratch_shapes=[pltpu.VMEM((half + 8, 2 * GROUP_W), F32), pltpu.VMEM((rows, GROUP_W), F32)],
        compiler_params=_cparams("parallel"),
        name="dft_fold",
    )(cq, sq, rev, zab)


def _attn_kernel(qt_ref, *refs, seg_lens):
    n_seg = len(seg_lens)
    kv_refs = refs[:2 * n_seg]
    o_ref = refs[2 * n_seg]
    acc_ref = refs[2 * n_seg + 1]
    tq = qt_ref.shape[2]
    chains = [(qs, hd) for qs in range(0, tq, TQ_SUB) for hd in range(HEADS)]
    items = []
    for s, lk in enumerate(seg_lens):
        chunk = min(KEY_CHUNK, lk)
        for c0 in range(0, lk, chunk):
            items += [(s, c0, chunk, ch) for ch in range(len(chains))]
    m = [None] * len(chains)
    acc = [None] * len(chains)
    scores = {}
    for i in range(len(items) + SCORE_LOOKAHEAD):
        if i < len(items):
            s, c0, chunk, ch = items[i]
            qs, hd = chains[ch]
            k = kv_refs[2 * s][0, c0:c0 + chunk, hd * HEAD_PAD:(hd + 1) * HEAD_PAD]
            scores[i] = _dot(k, qt_ref[0, hd * HEAD_PAD:(hd + 1) * HEAD_PAD, qs:qs + TQ_SUB])
        j = i - SCORE_LOOKAHEAD
        if j < 0:
            continue
        s, c0, chunk, ch = items[j]
        qs, hd = chains[ch]
        sc = scores.pop(j)
        cm = sc.max(axis=0, keepdims=True)
        m_new = cm if m[ch] is None else jnp.maximum(m[ch], cm)
        p = jnp.exp2(sc - m_new).astype(BF16)
        vt = kv_refs[2 * s + 1][0, hd * V_PAD:(hd + 1) * V_PAD, c0:c0 + chunk]
        pv = _dot(vt, p)
        acc[ch] = pv if acc[ch] is None else acc[ch] * jnp.exp2(m[ch] - m_new) + pv
        m[ch] = m_new
    for ch, (qs, hd) in enumerate(chains):
        inv = 1.0 / acc[ch][HEAD_W:HEAD_W + 1]
        acc_ref[hd * HEAD_W:(hd + 1) * HEAD_W, qs:qs + TQ_SUB] = acc[ch][:HEAD_W] * inv
    o_ref[0] = acc_ref[...].T.astype(BF16)


def _attention(qt, segments, *, tq):
    b, _, l = qt.shape
    in_specs = [pl.BlockSpec((1, HEADS * HEAD_PAD, tq), lambda i, j: (i, 0, j))]
    args = [qt]
    for k, vt in segments:
        lk = k.shape[1]
        in_specs.append(pl.BlockSpec((1, lk, HEADS * HEAD_PAD), lambda i, j: (i, 0, 0)))
        in_specs.append(pl.BlockSpec((1, HEADS * V_PAD, lk), lambda i, j: (i, 0, 0)))
        args += [k, vt]
    return pl.pallas_call(
        functools.partial(_attn_kernel, seg_lens=tuple(k.shape[1] for k, _ in segments)),
        out_shape=jax.ShapeDtypeStruct((b, l, GROUP_W), BF16),
        grid=(b, l // tq),
        in_specs=in_specs,
        out_specs=pl.BlockSpec((1, tq, GROUP_W), lambda i, j: (i, j, 0)),
        scratch_shapes=[pltpu.VMEM((GROUP_W, tq), F32)],
        compiler_params=_cparams("parallel", "parallel"),
        name="attention",
    )(*args)


def _pool_kernel(z_ref, wp_ref, sp_ref, o_ref, pad_ref, *, l, rb):
    zeros = jnp.zeros((POOL_PAD, GROUP_W), F32)
    pad_ref[0:POOL_PAD, :] = zeros
    pad_ref[POOL_PAD + l:2 * POOL_PAD + l, :] = zeros
    pad_ref[POOL_PAD:POOL_PAD + l, :] = z_ref[0]
    halo = MAX_HALF_WINDOW
    n = rb + 2 * halo
    lane = lax.broadcasted_iota(jnp.int32, (rb, HEAD_PAD), 1)
    first_group = lane < HEAD_W

    def up(a, kk):
        return pltpu.roll(a, n - kk, axis=0)

    def down(a, kk):
        return pltpu.roll(a, kk, axis=0)

    for r in range(l // rb):
        base = POOL_PAD + r * rb - halo
        edge = r == 0 or r == l // rb - 1
        pooled = []
        for tile in range(GROUP_W // HEAD_PAD):
            lanes = slice(tile * HEAD_PAD, (tile + 1) * HEAD_PAD)
            w_a, w_b = POOL_WINDOWS[2 * tile], POOL_WINDOWS[2 * tile + 1]
            x = pad_ref[base:base + n, lanes]
            run = {1: x}
            width = 1
            while width < min(w_b, halo):
                run[2 * width] = run[width] + up(run[width], width)
                width *= 2

            def window(w):
                if w in run:
                    return down(run[w], w // 2)
                return down(run[w // 2], w // 2) + run[w // 2]

            win = jnp.where(first_group, window(w_a)[halo:halo + rb], window(w_b)[halo:halo + rb])
            z = x[halo:halo + rb]
            if edge:
                t = lax.broadcasted_iota(jnp.int32, (rb, HEAD_PAD), 0) + r * rb
                hw = jnp.where(first_group, w_a // 2, w_b // 2)
                cnt = jnp.minimum(t + hw, l) - jnp.maximum(t - hw, 0)
                pooled.append(win / cnt.astype(F32) - z)
            else:
                pooled.append(win * jnp.where(first_group, 1.0 / w_a, 1.0 / w_b) - z)
        y = _dot(jnp.concatenate(pooled, axis=1).astype(BF16), wp_ref[...]) * sp_ref[...]
        o_ref[0, r * rb:(r + 1) * rb, :] = y.astype(BF16)


def _pool(zp, wp_bd, s_pool):
    b, l, _ = zp.shape
    rb = min(l, 256)
    return pl.pallas_call(
        functools.partial(_pool_kernel, l=l, rb=rb),
        out_shape=jax.ShapeDtypeStruct((b, l, GROUP_W), BF16),
        grid=(b,),
        in_specs=[
            pl.BlockSpec((1, l, GROUP_W), lambda i: (i, 0, 0)),
            _const_spec((GROUP_W, GROUP_W)),
            _const_spec((1, GROUP_W)),
        ],
        out_specs=pl.BlockSpec((1, l, GROUP_W), lambda i: (i, 0, 0)),
        scratch_shapes=[pltpu.VMEM((l + 2 * POOL_PAD, GROUP_W), F32)],
        compiler_params=_cparams("parallel"),
        name="pool",
    )(zp, wp_bd, s_pool)


def _block_diag(blocks):
    n, r, c = blocks.shape
    out = jnp.zeros((n * r, n * c), blocks.dtype)
    for i in range(n):
        out = lax.dynamic_update_slice(out, blocks[i], (i * r, i * c))
    return out


def _channel_dft():
    k = np.arange(HEAD_W)
    ang = 2.0 * np.pi * np.outer(k, k) / HEAD_W
    eye = np.eye(HEADS)
    c = np.kron(eye, np.cos(ang)) / math.sqrt(HEAD_W)
    s = np.kron(eye, np.sin(ang)) / math.sqrt(HEAD_W)
    return jnp.asarray(np.concatenate([c, s], axis=1), F32).astype(BF16)


def _token_dft(l, rows=None, cols=None):
    kk = jnp.arange(l if rows is None else rows, dtype=jnp.int32)
    tt = jnp.arange(l if cols is None else cols, dtype=jnp.int32)
    ang = ((kk[:, None] * tt[None, :]) % l).astype(F32) * (2.0 * math.pi / l)
    norm = 1.0 / math.sqrt(l)
    return (jnp.cos(ang) * norm).astype(BF16), (jnp.sin(ang) * norm).astype(BF16)


def _dft_table_kernel(ca_ref, sa_ref, cb_ref, sb_ref, c_ref, s_ref):
    ca, sa = ca_ref[0], sa_ref[0]
    cb, sb = cb_ref[...], sb_ref[...]
    c_ref[...] = (ca * cb - sa * sb).astype(BF16)
    s_ref[...] = (sa * cb + ca * sb).astype(BF16)


def _folded_token_dft(l):
    half = l // 2
    blocks = pl.cdiv(half + 1, TABLE_ROWS)
    tt = jnp.arange(half, dtype=jnp.int32)[None, :]
    angle = lambda kk: ((kk[:, None] * tt) % l).astype(F32) * (2.0 * math.pi / l)
    coarse = angle(jnp.arange(blocks, dtype=jnp.int32) * TABLE_ROWS)
    fine = angle(jnp.arange(TABLE_ROWS, dtype=jnp.int32))
    norm = 1.0 / math.sqrt(l)
    row_spec = pl.BlockSpec((1, 1, half), lambda i: (i, 0, 0))
    out_spec = pl.BlockSpec((TABLE_ROWS, half), lambda i: (i, 0))
    table = jax.ShapeDtypeStruct((blocks * TABLE_ROWS, half), BF16)
    cq, sq = pl.pallas_call(
        _dft_table_kernel,
        out_shape=[table, table],
        grid=(blocks,),
        in_specs=[row_spec, row_spec, _const_spec((TABLE_ROWS, half)), _const_spec((TABLE_ROWS, half))],
        out_specs=[out_spec, out_spec],
        compiler_params=_cparams("parallel"),
        name="dft_tables",
    )(jnp.cos(coarse)[:, None, :], jnp.sin(coarse)[:, None, :], jnp.cos(fine) * norm, jnp.sin(fine) * norm)
    rev = jnp.asarray(np.eye(FLIP)[::-1].copy(), F32).astype(BF16)
    return cq, sq, rev


def _rope_tables(l):
    pos = jnp.arange(l)
    row = (pos // GRID_W).astype(F32)
    col = (pos % GRID_W).astype(F32)
    inv = jnp.power(ROPE_BASE, -jnp.arange(0, AXIS_ROPE, 2, dtype=F32) / AXIS_ROPE)
    zero = jnp.zeros((l, ROPE_PAIR), F32)
    row_sin, col_sin = jnp.sin(row[:, None] * inv), jnp.sin(col[:, None] * inv)
    row_cos, col_cos = jnp.cos(row[:, None] * inv), jnp.cos(col[:, None] * inv)
    place = lambda base, parts: base.at[:, ROPE_LANE0:ROPE_LANE0 + QK_ROPE].set(jnp.concatenate(parts, axis=1))
    cos = place(jnp.ones((l, HEAD_PAD), F32), [row_cos, row_cos, col_cos, col_cos])
    sin_lo = place(jnp.zeros((l, HEAD_PAD), F32), [-row_sin, zero, -col_sin, zero])
    sin_hi = place(jnp.zeros((l, HEAD_PAD), F32), [zero, row_sin, zero, col_sin])
    return cos, sin_lo, sin_hi


def _identity_rope_tables(l):
    zero = jnp.zeros((l, HEAD_PAD), F32)
    return jnp.ones((l, HEAD_PAD), F32), zero, zero


def _layer_params(i, w13_ffn1, w2_ffn1, g_ffn1, g_mix, w_in, w_fnet, g_q, w_uq, g_kv, w_ukv,
                  g_sgu, w_sgu, b_sgu, w_pool, s_pool, w_out, g_ffn2, w13_ffn2, w2_ffn2):
    p = {}
    for tag, w13, w2, g in (("1", w13_ffn1, w2_ffn1, g_ffn1), ("2", w13_ffn2, w2_ffn2, g_ffn2)):
        p["w13_" + tag] = w13[i].astype(BF16)
        p["w2_" + tag] = w2[i].astype(BF16)
        p["g_" + tag] = g[i].reshape(1, D_MODEL)
    wi = w_in[i]
    p["g_mix"] = g_mix[i].reshape(1, D_MODEL)
    zcols = lambda n: jnp.zeros((D_MODEL, n), F32)
    p["w_cat"] = jnp.concatenate([
        wi[:, OFF_F:OFF_Q], wi[:, OFF_G:OFF_P], wi[:, OFF_P:],
        wi[:, OFF_Q:OFF_KV], zcols(Q_PAD - Q_RANK), wi[:, OFF_KV:OFF_KR],
        zcols(ROPE_LANE0), wi[:, OFF_KR:OFF_G], zcols(HEAD_PAD - ROPE_LANE0 - QK_ROPE)], axis=1).astype(BF16)
    p["dft_c"] = _channel_dft()
    p["g_q"] = jnp.concatenate([g_q[i], jnp.zeros((Q_PAD - Q_RANK,), F32)]).reshape(1, Q_PAD)
    wuq = w_uq[i].reshape(Q_RANK, HEADS, QK_NOPE + QK_ROPE)
    wuq = jnp.concatenate([wuq, jnp.zeros((Q_RANK, HEADS, HEAD_PAD - QK_NOPE - QK_ROPE), F32)], axis=-1)
    p["w_uq"] = jnp.concatenate([wuq.reshape(Q_RANK, HEADS * HEAD_PAD),
                                 jnp.zeros((Q_PAD - Q_RANK, HEADS * HEAD_PAD), F32)], axis=0).astype(BF16)
    p["g_kv"] = g_kv[i].reshape(1, KV_RANK)
    wukv = w_ukv[i].reshape(KV_RANK, HEADS, QK_NOPE + HEAD_W)
    p["w_uk"] = jnp.concatenate([wukv[..., :QK_NOPE], jnp.zeros((KV_RANK, HEADS, HEAD_PAD - QK_NOPE), F32)],
                                axis=-1).reshape(KV_RANK, HEADS * HEAD_PAD).astype(BF16)
    p["w_uv"] = jnp.concatenate([wukv[..., QK_NOPE:], jnp.zeros((KV_RANK, HEADS, V_PAD - HEAD_W), F32)],
                                axis=-1).reshape(KV_RANK, HEADS * V_PAD).astype(BF16)
    p["v_one"] = jnp.zeros((1, HEADS, V_PAD), F32).at[:, :, HEAD_W].set(1.0).reshape(1, HEADS * V_PAD)
    p["g_sgu"] = g_sgu[i].reshape(1, GROUP_W)
    p["ones_bd"] = jnp.asarray(np.kron(np.eye(HEADS), np.full((HEAD_W, HEAD_W), 1.0 / HEAD_W)), BF16)
    p["w_sgu"] = w_sgu[i].reshape(HEADS * SGU_CHUNK, SGU_CHUNK).astype(BF16)
    p["b_sgu"] = jnp.repeat(b_sgu[i].T, HEAD_W, axis=1)
    p["wf_bd"] = _block_diag(w_fnet[i]).astype(BF16)
    p["wp_bd"] = _block_diag(w_pool[i]).astype(BF16)
    p["s_pool"] = s_pool[i].reshape(1, GROUP_W)
    p["w_out"] = w_out[i].astype(BF16)
    return p


def _mix(h, mods, rope, dft, kv_extra, p, *, tm, tq):
    zab, q, k, v, sgu, zp = _inproj(h, mods, rope, p, kv_only=False, tm=tm)
    if len(dft) == 3:
        fre = _dft_fold(zab, *dft)
    else:
        fre = _dft(zab, dft[0], dft[1], tm=min(512, h.shape[1]))
    att = _attention(q, [(k, v)] + kv_extra, tq=tq)
    pool = _pool(zp, p["wp_bd"], p["s_pool"])
    return (fre, att, sgu, pool, p["wf_bd"], p["w_out"]), (k, v)


def kernel(x, c, ctx, c_ctx, w_ada, b_ada, g_ffn1, w13_ffn1, w2_ffn1, g_mix, w_in, w_fnet, g_q, w_uq, g_kv, w_ukv, g_sgu, w_sgu, b_sgu, w_pool, s_pool, w_out, g_ffn2, w13_ffn2, w2_ffn2, g_final):
    b, l, d = x.shape
    lc = ctx.shape[1]
    depth = w_ada.shape[0]
    tm = 512
    tm_ffn = 1024
    tmc = min(lc, 256)

    cond_rows = 16
    cond = jnp.zeros((cond_rows, d), F32).at[:b].set(c).at[b].set(c_ctx)
    mods_all = _ada(cond, w_ada, b_ada)

    rope = _rope_tables(l)
    rope_c = _identity_rope_tables(lc)
    dft = _folded_token_dft(l)
    dft_c = _token_dft(lc)

    h, hc = x, ctx
    for i in range(depth):
        last = i == depth - 1
        p = _layer_params(i, w13_ffn1, w2_ffn1, g_ffn1, g_mix, w_in, w_fnet, g_q, w_uq, g_kv, w_ukv,
                          g_sgu, w_sgu, b_sgu, w_pool, s_pool, w_out, g_ffn2, w13_ffn2, w2_ffn2)
        m = mods_all[i, :b].reshape(b, N_MOD, d)
        mc = mods_all[i, b:b + 1].reshape(1, N_MOD, d)

        h = _ffn(h, m, p["g_1"], p["w13_1"], p["w2_1"], row0=0, tm=tm_ffn)
        hc = _ffn(hc, mc, p["g_1"], p["w13_1"], p["w2_1"], row0=0, tm=tmc)

        if last:
            kc, vc = _inproj(hc, mc, rope_c, p, kv_only=True, tm=tmc)
        else:
            mix_c, (kc, vc) = _mix(hc, mc, rope_c, dft_c, [], p, tm=tmc, tq=tmc)
        mix, _ = _mix(h, m, rope, dft, [(kc, vc)], p, tm=tm, tq=TQ)
        h = _ffn(h, m, p["g_2"], p["w13_2"], p["w2_2"], row0=6, tm=tm_ffn, mix=mix,
                 g_final=g_final.reshape(1, d) if last else None)
        if not last:
            hc = _ffn(hc, mc, p["g_2"], p["w13_2"], p["w2_2"], row0=6, tm=tmc, mix=mix_c)
    return h
```

```python
import functools
import math

import numpy as np
import jax
import jax.numpy as jnp
from jax import lax
from jax.experimental import pallas as pl
from jax.experimental.pallas import tpu as pltpu

F32 = jnp.float32
BF16 = jnp.bfloat16

D_MODEL = 1024
DEPTH = 2
GRID_W = 64
EPS = 1e-6
N_MOD = 9
GROUP_W = 256
HEADS = 4
HEAD_W = GROUP_W // HEADS
QK_NOPE = 64
QK_ROPE = 32
AXIS_ROPE = QK_ROPE // 2
Q_RANK = 192
KV_RANK = 128
ROPE_BASE = 10000.0
SGU_CHUNK = 128
POOL_WINDOWS = (2, 4, 8, 16)
D_FF = 2816
HEAD_PAD = 128
V_PAD = 96
KEY_CHUNK = 256
SCORE_LOOKAHEAD = 12
FLIP = 256
TABLE_ROWS = 128
ROPE_LANE0 = QK_NOPE

OFF_F = 0
OFF_Q = OFF_F + GROUP_W
OFF_KV = OFF_Q + Q_RANK
OFF_KR = OFF_KV + KV_RANK
OFF_G = OFF_KR + QK_ROPE
OFF_P = OFF_G + 2 * GROUP_W
IN_W = OFF_P + GROUP_W

CAT_F = 0
CAT_G = CAT_F + GROUP_W
CAT_P = CAT_G + 2 * GROUP_W
CAT_Q = CAT_P + GROUP_W
Q_PAD = 256
CAT_KV = CAT_Q + Q_PAD
CAT_KR = CAT_KV + KV_RANK
CAT_W = CAT_KR + HEAD_PAD
ROPE_PAIR = AXIS_ROPE // 2
TQ = 512
TQ_SUB = 256

MAX_HALF_WINDOW = max(POOL_WINDOWS) // 2
POOL_PAD = 16
VMEM_LIMIT = 56 * 1024 * 1024


def _cparams(*sem):
    return pltpu.CompilerParams(dimension_semantics=sem, vmem_limit_bytes=VMEM_LIMIT)


def _const_spec(shape):
    nd = len(shape)
    return pl.BlockSpec(shape, lambda *_: (0,) * nd, pipeline_mode=pl.Buffered(1))


def _layer_spec(shape, layer):
    nd = len(shape)
    return pl.BlockSpec((1,) + tuple(shape), lambda *_: (layer,) + (0,) * nd, pipeline_mode=pl.Buffered(1))


def _dot(a, b):
    return jnp.dot(a, b, preferred_element_type=F32)


def _rms(x):
    return x * lax.rsqrt(jnp.mean(x * x, axis=-1, keepdims=True) + EPS)


def _norm_mod(x, g, shift, scale):
    return (_rms(x) * g) * (1.0 + scale) + shift


def _ada_kernel(c_ref, w_ref, b_ref, o_ref):
    c = c_ref[...]
    s = c * jax.nn.sigmoid(c)
    hi = s.astype(BF16)
    lo = (s - hi.astype(F32)).astype(BF16)
    r = _dot(jnp.concatenate([hi, lo], axis=0), w_ref[0].astype(BF16))
    rows = c.shape[0]
    o_ref[0] = r[:rows] + r[rows:] + b_ref[0]


def _ada(cond, w_ada, b_ada):
    rows, d = cond.shape
    depth, _, width = w_ada.shape
    bn = 1152
    return pl.pallas_call(
        _ada_kernel,
        out_shape=jax.ShapeDtypeStruct((depth, rows, width), F32),
        grid=(depth, width // bn),
        in_specs=[
            pl.BlockSpec((rows, d), lambda i, j: (0, 0)),
            pl.BlockSpec((1, d, bn), lambda i, j: (i, 0, j)),
            pl.BlockSpec((1, 1, bn), lambda i, j: (i, 0, j)),
        ],
        out_specs=pl.BlockSpec((1, rows, bn), lambda i, j: (i, 0, j)),
        compiler_params=_cparams("parallel", "parallel"),
        name="ada",
    )(cond, w_ada, b_ada.reshape(depth, 1, width))


def _ffn_kernel(h_ref, mod_ref, g_ref, w13_ref, w2_ref, *rest, row0, fchunk, mix_in, final_norm):
    rest = list(rest)
    mix_refs = [rest.pop(0) for _ in range(6)] if mix_in else None
    gf_ref = rest.pop(0) if final_norm else None
    o_ref, xn_ref, act_ref = rest
    shift = mod_ref[0, row0:row0 + 1, :]
    scale = mod_ref[0, row0 + 1:row0 + 2, :]
    gate = mod_ref[0, row0 + 2:row0 + 3, :]
    x = h_ref[0]
    if mix_in:
        f_ref, att_ref, sgu_ref, pool_ref, wf_ref, wo_ref = mix_refs
        fmix = _dot(f_ref[0], wf_ref[...]).astype(BF16)
        mixed = jnp.concatenate([fmix, att_ref[0], sgu_ref[0], pool_ref[0]], axis=1)
        x = x + mod_ref[0, 5:6, :] * _dot(mixed, wo_ref[0])
    xn_ref[...] = _norm_mod(x, g_ref[...], shift, scale).astype(BF16)
    for j in range(D_FF // fchunk):
        sl = slice(j * fchunk, (j + 1) * fchunk)
        a = _dot(xn_ref[...], w13_ref[0, :, sl])
        b = _dot(xn_ref[...], w13_ref[0, :, D_FF + j * fchunk:D_FF + (j + 1) * fchunk])
        act_ref[:, sl] = (a * jax.nn.sigmoid(a) * b).astype(BF16)
    y = _dot(act_ref[...], w2_ref[0])
    out = x + (0.5 * gate) * y
    if final_norm:
        out = _rms(out) * gf_ref[...]
    o_ref[0] = out


def _ffn(h, mods, g, w13, w2, layer, *, row0, tm, mix=None, g_final=None):
    b, l, d = h.shape
    per_batch = mods.shape[0] > 1
    kern = functools.partial(_ffn_kernel, row0=row0, fchunk=256,
                             mix_in=mix is not None, final_norm=g_final is not None)
    extra, extra_specs = [], []
    if mix is not None:
        extra += list(mix)
        extra_specs += [pl.BlockSpec((1, tm, GROUP_W), lambda i, j: (i, j, 0))] * 4
        extra_specs += [_const_spec((GROUP_W, GROUP_W)), _layer_spec((4 * GROUP_W, d), layer)]
    if g_final is not None:
        extra.append(g_final)
        extra_specs.append(_const_spec((1, d)))
    return pl.pallas_call(
        kern,
        out_shape=jax.ShapeDtypeStruct(h.shape, F32),
        grid=(b, l // tm),
        in_specs=[
            pl.BlockSpec((1, tm, d), lambda i, j: (i, j, 0)),
            pl.BlockSpec((1, N_MOD, d), (lambda i, j: (i, 0, 0)) if per_batch else (lambda i, j: (0, 0, 0))),
            _const_spec((1, d)),
            _layer_spec((d, 2 * D_FF), layer),
            _layer_spec((D_FF, d), layer),
        ] + extra_specs,
        out_specs=pl.BlockSpec((1, tm, d), lambda i, j: (i, j, 0)),
        scratch_shapes=[pltpu.VMEM((tm, d), BF16), pltpu.VMEM((tm, D_FF), BF16)],
        compiler_params=_cparams("parallel", "parallel"),
        name="ffn",
    )(h, mods, g, w13, w2, *extra)


def _gelu_tanh(x):
    return 0.5 * x * (1.0 + jnp.tanh(0.7978845608028654 * (x + 0.044715 * (x * x * x))))


def _group_mean(x, ones_bd):
    hi = x.astype(BF16)
    lo = (x - hi.astype(F32)).astype(BF16)
    return _dot(hi, ones_bd) + _dot(lo, ones_bd)


def _rope(x, cos, sin_lo, sin_hi):
    n = x.shape[1]
    up = pltpu.roll(x, n - ROPE_PAIR, axis=1)
    down = pltpu.roll(x, ROPE_PAIR, axis=1)
    return x * cos + up * sin_lo + down * sin_hi


def _inproj_kernel(h_ref, mod_ref, g_ref, cos_ref, slo_ref, shi_ref,
                   wcat_ref, dftc_ref, gq_ref, wuq_ref, gkv_ref, wuk_ref, wuv_ref, vone_ref,
                   gsgu_ref, ones_ref, wsgu_ref, bsgu_ref,
                   *out_refs, kv_only, tm, q_scale):
    if kv_only:
        k_ref, vt_ref = out_refs
    else:
        zab_ref, qt_ref, k_ref, vt_ref, sgu_ref, zp_ref = out_refs
    shift = mod_ref[0, 3:4, :]
    scale = mod_ref[0, 4:5, :]
    n = _norm_mod(h_ref[0], g_ref[...], shift, scale).astype(BF16)
    rope = (cos_ref[...], slo_ref[...], shi_ref[...])

    def proj(c0, c1):
        return _dot(n, wcat_ref[:, c0:c1])

    zkv = proj(CAT_KV, CAT_W)
    if not kv_only:
        zq = proj(CAT_Q, CAT_KV)

    kvn = (_rms(zkv[:, :KV_RANK]) * gkv_ref[...]).astype(BF16)
    k = _dot(kvn, wuk_ref[...]) + jnp.concatenate([_rope(zkv[:, KV_RANK:], *rope)] * HEADS, axis=1)
    k_ref[0] = k.astype(BF16)
    vt_ref[0] = (_dot(kvn, wuv_ref[...]) + vone_ref[...]).T.astype(BF16)
    if kv_only:
        return

    zg = proj(CAT_G, CAT_P)

    ms = jnp.sum(zq * zq, axis=-1, keepdims=True) * (1.0 / Q_RANK)
    qn = (zq * lax.rsqrt(ms + EPS) * gq_ref[...]).astype(BF16)
    rope4 = [jnp.concatenate([t] * HEADS, axis=1) for t in rope]
    q = _rope(_dot(qn, wuq_ref[...]), *rope4)
    qt_ref[0] = (q * q_scale).T.astype(BF16)

    zf = proj(CAT_F, CAT_G)
    zp_ref[0] = proj(CAT_P, CAT_Q)

    zab_ref[0] = _dot(zf.astype(BF16), dftc_ref[...]).astype(BF16)

    gz = _gelu_tanh(zg)
    u = gz[:, :GROUP_W]
    vv = gz[:, GROUP_W:]
    ms = _group_mean(vv * vv, ones_ref[...])
    vn = (vv * lax.rsqrt(ms + EPS) * gsgu_ref[...]).astype(BF16)
    lane_head = lax.broadcasted_iota(jnp.int32, (SGU_CHUNK, GROUP_W), 1) // HEAD_W
    for c in range(tm // SGU_CHUNK):
        rows = slice(c * SGU_CHUNK, (c + 1) * SGU_CHUNK)
        r = _dot(wsgu_ref[...], vn[rows])
        sel = r[0:SGU_CHUNK]
        for hd in range(1, HEADS):
            sel = jnp.where(lane_head == hd, r[hd * SGU_CHUNK:(hd + 1) * SGU_CHUNK], sel)
        sgu_ref[0, rows, :] = (u[rows] * (sel + bsgu_ref[...])).astype(BF16)


def _inproj(h, mods, rope, p, *, kv_only, tm):
    b, l, d = h.shape
    per_batch = mods.shape[0] > 1
    weights = [p["w_cat"], p["dft_c"], p["g_q"], p["w_uq"], p["g_kv"], p["w_uk"], p["w_uv"], p["v_one"],
               p["g_sgu"], p["ones_bd"], p["w_sgu"], p["b_sgu"]]
    rope_spec = pl.BlockSpec((tm, HEAD_PAD), lambda i, j: (j, 0))
    tok = lambda w, dt: jax.ShapeDtypeStruct((b, l, w), dt)
    tok_spec = lambda w: pl.BlockSpec((1, tm, w), lambda i, j: (i, j, 0))
    tr = lambda w: jax.ShapeDtypeStruct((b, w, l), BF16)
    tr_spec = lambda w: pl.BlockSpec((1, w, tm), lambda i, j: (i, 0, j))
    if kv_only:
        out_shape = [tok(HEADS * HEAD_PAD, BF16), tr(HEADS * V_PAD)]
        out_specs = [tok_spec(HEADS * HEAD_PAD), tr_spec(HEADS * V_PAD)]
    else:
        out_shape = [tok(2 * GROUP_W, BF16), tr(HEADS * HEAD_PAD), tok(HEADS * HEAD_PAD, BF16),
                     tr(HEADS * V_PAD), tok(GROUP_W, BF16), tok(GROUP_W, F32)]
        out_specs = [tok_spec(2 * GROUP_W), tr_spec(HEADS * HEAD_PAD), tok_spec(HEADS * HEAD_PAD),
                     tr_spec(HEADS * V_PAD), tok_spec(GROUP_W), tok_spec(GROUP_W)]
    kern = functools.partial(_inproj_kernel, kv_only=kv_only, tm=tm,
                             q_scale=float(QK_NOPE + QK_ROPE) ** -0.5 * math.log2(math.e))
    return pl.pallas_call(
        kern,
        out_shape=out_shape,
        grid=(b, l // tm),
        in_specs=[
            pl.BlockSpec((1, tm, d), lambda i, j: (i, j, 0)),
            pl.BlockSpec((1, N_MOD, d), (lambda i, j: (i, 0, 0)) if per_batch else (lambda i, j: (0, 0, 0))),
            _const_spec((1, d)),
            rope_spec, rope_spec, rope_spec,
        ] + [_const_spec(w.shape) for w in weights],
        out_specs=out_specs,
        compiler_params=_cparams("parallel", "parallel"),
        name="inproj",
    )(h, mods, p["g_mix"], *rope, *weights)


def _dft_kernel(c_ref, s_ref, z_ref, o_ref):
    za = z_ref[0, :, :GROUP_W]
    zb = z_ref[0, :, GROUP_W:]
    o_ref[0] = (_dot(c_ref[...], za) - _dot(s_ref[...], zb)).astype(BF16)


def _dft(zab, cmat, smat, *, tm):
    b, l, _ = zab.shape
    return pl.pallas_call(
        _dft_kernel,
        out_shape=jax.ShapeDtypeStruct((b, l, GROUP_W), BF16),
        grid=(l // tm, b),
        in_specs=[
            pl.BlockSpec((tm, l), lambda i, j: (i, 0)),
            pl.BlockSpec((tm, l), lambda i, j: (i, 0)),
            pl.BlockSpec((1, l, 2 * GROUP_W), lambda i, j: (j, 0, 0)),
        ],
        out_specs=pl.BlockSpec((1, tm, GROUP_W), lambda i, j: (j, i, 0)),
        compiler_params=_cparams("parallel", "parallel"),
        name="dft",
    )(cmat, smat, zab)


def _dft_fold_kernel(cq_ref, sq_ref, rev_ref, z_ref, o_ref, fold_ref, t_ref, *, l):
    half = l // 2
    nb = half // FLIP
    rev = rev_ref[...]
    fold_ref[0:8, :] = jnp.zeros((8, 2 * GROUP_W), F32)
    for jb in range(nb):
        blk = z_ref[0, (2 * nb - 1 - jb) * FLIP:(2 * nb - jb) * FLIP, :]
        fold_ref[8 + jb * FLIP:8 + (jb + 1) * FLIP, :] = _dot(rev, blk)
    zlo = z_ref[0, 0:half, :].astype(F32)
    mirrored = fold_ref[7:7 + half, :]
    even = (zlo[:, :GROUP_W] + mirrored[:, :GROUP_W]).astype(BF16)
    odd = (zlo[:, GROUP_W:] - mirrored[:, GROUP_W:]).astype(BF16)
    p = _dot(cq_ref[...], even)
    q = _dot(sq_ref[...], odd)
    k_idx = lax.broadcasted_iota(jnp.int32, (p.shape[0], 1), 0)
    sign = (1 - 2 * (k_idx & 1)).astype(F32) * (1.0 / math.sqrt(l))
    p = p + sign * z_ref[0, half:half + 1, :GROUP_W].astype(F32)
    o_ref[0, 0:half, :] = (p[:half] - q[:half]).astype(BF16)
    t_ref[...] = p + q
    upper = t_ref[1:half + 1, :].astype(BF16)
    for jb in range(nb):
        blk = upper[(nb - 1 - jb) * FLIP:(nb - jb) * FLIP, :]
        o_ref[0, half + jb * FLIP:half + (jb + 1) * FLIP, :] = _dot(rev, blk).astype(BF16)


def _dft_fold(zab, cq, sq, rev):
    b, l, _ = zab.shape
    half = l // 2
    rows = half + 16
    return pl.pallas_call(
        functools.partial(_dft_fold_kernel, l=l),
        out_shape=jax.ShapeDtypeStruct((b, l, GROUP_W), BF16),
        grid=(b,),
        in_specs=[
            _const_spec((rows, half)),
            _const_spec((rows, half)),
            _const_spec((FLIP, FLIP)),
            pl.BlockSpec((1, l, 2 * GROUP_W), lambda i: (i, 0, 0)),
        ],
        out_specs=pl.BlockSpec((1, l, GROUP_W), lambda i: (i, 0, 0)),
        scratch_shapes=[pltpu.VMEM((half + 8, 2 * GROUP_W), F32), pltpu.VMEM((rows, GROUP_W), F32)],
        compiler_params=_cparams("parallel"),
        name="dft_fold",
    )(cq, sq, rev, zab)


def _attn_kernel(qt_ref, *refs, seg_lens):
    n_seg = len(seg_lens)
    kv_refs = refs[:2 * n_seg]
    o_ref = refs[2 * n_seg]
    acc_ref = refs[2 * n_seg + 1]
    tq = qt_ref.shape[2]
    chains = [(qs, hd) for qs in range(0, tq, TQ_SUB) for hd in range(HEADS)]
    items = []
    for s, lk in enumerate(seg_lens):
        chunk = min(KEY_CHUNK, lk)
        for c0 in range(0, lk, chunk):
            items += [(s, c0, chunk, ch) for ch in range(len(chains))]
    m = [None] * len(chains)
    acc = [None] * len(chains)
    scores = {}
    for i in range(len(items) + SCORE_LOOKAHEAD):
        if i < len(items):
            s, c0, chunk, ch = items[i]
            qs, hd = chains[ch]
            k = kv_refs[2 * s][0, c0:c0 + chunk, hd * HEAD_PAD:(hd + 1) * HEAD_PAD]
            scores[i] = _dot(k, qt_ref[0, hd * HEAD_PAD:(hd + 1) * HEAD_PAD, qs:qs + TQ_SUB])
        j = i - SCORE_LOOKAHEAD
        if j < 0:
            continue
        s, c0, chunk, ch = items[j]
        qs, hd = chains[ch]
        sc = scores.pop(j)
        cm = sc.max(axis=0, keepdims=True)
        m_new = cm if m[ch] is None else jnp.maximum(m[ch], cm)
        p = jnp.exp2(sc - m_new).astype(BF16)
        vt = kv_refs[2 * s + 1][0, hd * V_PAD:(hd + 1) * V_PAD, c0:c0 + chunk]
        pv = _dot(vt, p)
        acc[ch] = pv if acc[ch] is None else acc[ch] * jnp.exp2(m[ch] - m_new) + pv
        m[ch] = m_new
    for ch, (qs, hd) in enumerate(chains):
        inv = 1.0 / acc[ch][HEAD_W:HEAD_W + 1]
        acc_ref[hd * HEAD_W:(hd + 1) * HEAD_W, qs:qs + TQ_SUB] = acc[ch][:HEAD_W] * inv
    o_ref[0] = acc_ref[...].T.astype(BF16)


def _attention(qt, segments, *, tq):
    b, _, l = qt.shape
    in_specs = [pl.BlockSpec((1, HEADS * HEAD_PAD, tq), lambda i, j: (i, 0, j))]
    args = [qt]
    for k, vt in segments:
        lk = k.shape[1]
        in_specs.append(pl.BlockSpec((1, lk, HEADS * HEAD_PAD), lambda i, j: (i, 0, 0)))
        in_specs.append(pl.BlockSpec((1, HEADS * V_PAD, lk), lambda i, j: (i, 0, 0)))
        args += [k, vt]
    return pl.pallas_call(
        functools.partial(_attn_kernel, seg_lens=tuple(k.shape[1] for k, _ in segments)),
        out_shape=jax.ShapeDtypeStruct((b, l, GROUP_W), BF16),
        grid=(b, l // tq),
        in_specs=in_specs,
        out_specs=pl.BlockSpec((1, tq, GROUP_W), lambda i, j: (i, j, 0)),
        scratch_shapes=[pltpu.VMEM((GROUP_W, tq), F32)],
        compiler_params=_cparams("parallel", "parallel"),
        name="attention",
    )(*args)


def _pool_kernel(z_ref, wp_ref, sp_ref, o_ref, pad_ref, *, l, rb):
    zeros = jnp.zeros((POOL_PAD, GROUP_W), F32)
    pad_ref[0:POOL_PAD, :] = zeros
    pad_ref[POOL_PAD + l:2 * POOL_PAD + l, :] = zeros
    pad_ref[POOL_PAD:POOL_PAD + l, :] = z_ref[0]
    halo = MAX_HALF_WINDOW
    n = rb + 2 * halo
    lane = lax.broadcasted_iota(jnp.int32, (rb, HEAD_PAD), 1)
    first_group = lane < HEAD_W

    def up(a, kk):
        return pltpu.roll(a, n - kk, axis=0)

    def down(a, kk):
        return pltpu.roll(a, kk, axis=0)

    for r in range(l // rb):
        base = POOL_PAD + r * rb - halo
        edge = r == 0 or r == l // rb - 1
        pooled = []
        for tile in range(GROUP_W // HEAD_PAD):
            lanes = slice(tile * HEAD_PAD, (tile + 1) * HEAD_PAD)
            w_a, w_b = POOL_WINDOWS[2 * tile], POOL_WINDOWS[2 * tile + 1]
            x = pad_ref[base:base + n, lanes]
            run = {1: x}
            width = 1
            while width < min(w_b, halo):
                run[2 * width] = run[width] + up(run[width], width)
                width *= 2

            def window(w):
                if w in run:
                    return down(run[w], w // 2)
                return down(run[w // 2], w // 2) + run[w // 2]

            win = jnp.where(first_group, window(w_a)[halo:halo + rb], window(w_b)[halo:halo + rb])
            z = x[halo:halo + rb]
            if edge:
                t = lax.broadcasted_iota(jnp.int32, (rb, HEAD_PAD), 0) + r * rb
                hw = jnp.where(first_group, w_a // 2, w_b // 2)
                cnt = jnp.minimum(t + hw, l) - jnp.maximum(t - hw, 0)
                pooled.append(win / cnt.astype(F32) - z)
            else:
                pooled.append(win * jnp.where(first_group, 1.0 / w_a, 1.0 / w_b) - z)
        y = _dot(jnp.concatenate(pooled, axis=1).astype(BF16), wp_ref[...]) * sp_ref[...]
        o_ref[0, r * rb:(r + 1) * rb, :] = y.astype(BF16)


def _pool(zp, wp_bd, s_pool):
    b, l, _ = zp.shape
    rb = min(l, 256)
    return pl.pallas_call(
        functools.partial(_pool_kernel, l=l, rb=rb),
        out_shape=jax.ShapeDtypeStruct((b, l, GROUP_W), BF16),
        grid=(b,),
        in_specs=[
            pl.BlockSpec((1, l, GROUP_W), lambda i: (i, 0, 0)),
            _const_spec((GROUP_W, GROUP_W)),
            _const_spec((1, GROUP_W)),
        ],
        out_specs=pl.BlockSpec((1, l, GROUP_W), lambda i: (i, 0, 0)),
        scratch_shapes=[pltpu.VMEM((l + 2 * POOL_PAD, GROUP_W), F32)],
        compiler_params=_cparams("parallel"),
        name="pool",
    )(zp, wp_bd, s_pool)


def _block_diag(blocks):
    n, r, c = blocks.shape
    out = jnp.zeros((n * r, n * c), blocks.dtype)
    for i in range(n):
        out = lax.dynamic_update_slice(out, blocks[i], (i * r, i * c))
    return out


def _channel_dft():
    k = np.arange(HEAD_W)
    ang = 2.0 * np.pi * np.outer(k, k) / HEAD_W
    eye = np.eye(HEADS)
    c = np.kron(eye, np.cos(ang)) / math.sqrt(HEAD_W)
    s = np.kron(eye, np.sin(ang)) / math.sqrt(HEAD_W)
    return jnp.asarray(np.concatenate([c, s], axis=1), F32).astype(BF16)


def _token_dft(l, rows=None, cols=None):
    kk = jnp.arange(l if rows is None else rows, dtype=jnp.int32)
    tt = jnp.arange(l if cols is None else cols, dtype=jnp.int32)
    ang = ((kk[:, None] * tt[None, :]) % l).astype(F32) * (2.0 * math.pi / l)
    norm = 1.0 / math.sqrt(l)
    return (jnp.cos(ang) * norm).astype(BF16), (jnp.sin(ang) * norm).astype(BF16)


def _dft_table_kernel(ca_ref, sa_ref, cb_ref, sb_ref, c_ref, s_ref):
    ca, sa = ca_ref[0], sa_ref[0]
    cb, sb = cb_ref[...], sb_ref[...]
    c_ref[...] = (ca * cb - sa * sb).astype(BF16)
    s_ref[...] = (sa * cb + ca * sb).astype(BF16)


def _folded_token_dft(l):
    half = l // 2
    blocks = pl.cdiv(half + 1, TABLE_ROWS)
    tt = jnp.arange(half, dtype=jnp.int32)[None, :]
    angle = lambda kk: ((kk[:, None] * tt) % l).astype(F32) * (2.0 * math.pi / l)
    coarse = angle(jnp.arange(blocks, dtype=jnp.int32) * TABLE_ROWS)
    fine = angle(jnp.arange(TABLE_ROWS, dtype=jnp.int32))
    norm = 1.0 / math.sqrt(l)
    row_spec = pl.BlockSpec((1, 1, half), lambda i: (i, 0, 0))
    out_spec = pl.BlockSpec((TABLE_ROWS, half), lambda i: (i, 0))
    table = jax.ShapeDtypeStruct((blocks * TABLE_ROWS, half), BF16)
    cq, sq = pl.pallas_call(
        _dft_table_kernel,
        out_shape=[table, table],
        grid=(blocks,),
        in_specs=[row_spec, row_spec, _const_spec((TABLE_ROWS, half)), _const_spec((TABLE_ROWS, half))],
        out_specs=[out_spec, out_spec],
        compiler_params=_cparams("parallel"),
        name="dft_tables",
    )(jnp.cos(coarse)[:, None, :], jnp.sin(coarse)[:, None, :], jnp.cos(fine) * norm, jnp.sin(fine) * norm)
    rev = jnp.asarray(np.eye(FLIP)[::-1].copy(), F32).astype(BF16)
    return cq, sq, rev


def _rope_tables(l):
    pos = jnp.arange(l)
    row = (pos // GRID_W).astype(F32)
    col = (pos % GRID_W).astype(F32)
    inv = jnp.power(ROPE_BASE, -jnp.arange(0, AXIS_ROPE, 2, dtype=F32) / AXIS_ROPE)
    zero = jnp.zeros((l, ROPE_PAIR), F32)
    row_sin, col_sin = jnp.sin(row[:, None] * inv), jnp.sin(col[:, None] * inv)
    row_cos, col_cos = jnp.cos(row[:, None] * inv), jnp.cos(col[:, None] * inv)
    place = lambda base, parts: base.at[:, ROPE_LANE0:ROPE_LANE0 + QK_ROPE].set(jnp.concatenate(parts, axis=1))
    cos = place(jnp.ones((l, HEAD_PAD), F32), [row_cos, row_cos, col_cos, col_cos])
    sin_lo = place(jnp.zeros((l, HEAD_PAD), F32), [-row_sin, zero, -col_sin, zero])
    sin_hi = place(jnp.zeros((l, HEAD_PAD), F32), [zero, row_sin, zero, col_sin])
    return cos, sin_lo, sin_hi


def _identity_rope_tables(l):
    zero = jnp.zeros((l, HEAD_PAD), F32)
    return jnp.ones((l, HEAD_PAD), F32), zero, zero


def _layer_params(i, g_ffn1, g_mix, w_in, w_fnet, g_q, w_uq, g_kv, w_ukv,
                  g_sgu, w_sgu, b_sgu, w_pool, s_pool, g_ffn2):
    p = {}
    p["g_1"] = g_ffn1[i].reshape(1, D_MODEL)
    p["g_2"] = g_ffn2[i].reshape(1, D_MODEL)
    wi = w_in[i]
    p["g_mix"] = g_mix[i].reshape(1, D_MODEL)
    zcols = lambda n: jnp.zeros((D_MODEL, n), F32)
    p["w_cat"] = jnp.concatenate([
        wi[:, OFF_F:OFF_Q], wi[:, OFF_G:OFF_P], wi[:, OFF_P:],
        wi[:, OFF_Q:OFF_KV], zcols(Q_PAD - Q_RANK), wi[:, OFF_KV:OFF_KR],
        zcols(ROPE_LANE0), wi[:, OFF_KR:OFF_G], zcols(HEAD_PAD - ROPE_LANE0 - QK_ROPE)], axis=1).astype(BF16)
    p["dft_c"] = _channel_dft()
    p["g_q"] = jnp.concatenate([g_q[i], jnp.zeros((Q_PAD - Q_RANK,), F32)]).reshape(1, Q_PAD)
    wuq = w_uq[i].reshape(Q_RANK, HEADS, QK_NOPE + QK_ROPE)
    wuq = jnp.concatenate([wuq, jnp.zeros((Q_RANK, HEADS, HEAD_PAD - QK_NOPE - QK_ROPE), F32)], axis=-1)
    p["w_uq"] = jnp.concatenate([wuq.reshape(Q_RANK, HEADS * HEAD_PAD),
                                 jnp.zeros((Q_PAD - Q_RANK, HEADS * HEAD_PAD), F32)], axis=0).astype(BF16)
    p["g_kv"] = g_kv[i].reshape(1, KV_RANK)
    wukv = w_ukv[i].reshape(KV_RANK, HEADS, QK_NOPE + HEAD_W)
    p["w_uk"] = jnp.concatenate([wukv[..., :QK_NOPE], jnp.zeros((KV_RANK, HEADS, HEAD_PAD - QK_NOPE), F32)],
                                axis=-1).reshape(KV_RANK, HEADS * HEAD_PAD).astype(BF16)
    p["w_uv"] = jnp.concatenate([wukv[..., QK_NOPE:], jnp.zeros((KV_RANK, HEADS, V_PAD - HEAD_W), F32)],
                                axis=-1).reshape(KV_RANK, HEADS * V_PAD).astype(BF16)
    p["v_one"] = jnp.zeros((1, HEADS, V_PAD), F32).at[:, :, HEAD_W].set(1.0).reshape(1, HEADS * V_PAD)
    p["g_sgu"] = g_sgu[i].reshape(1, GROUP_W)
    p["ones_bd"] = jnp.asarray(np.kron(np.eye(HEADS), np.full((HEAD_W, HEAD_W), 1.0 / HEAD_W)), BF16)
    p["w_sgu"] = w_sgu[i].reshape(HEADS * SGU_CHUNK, SGU_CHUNK).astype(BF16)
    p["b_sgu"] = jnp.repeat(b_sgu[i].T, HEAD_W, axis=1)
    p["wf_bd"] = _block_diag(w_fnet[i]).astype(BF16)
    p["wp_bd"] = _block_diag(w_pool[i]).astype(BF16)
    p["s_pool"] = s_pool[i].reshape(1, GROUP_W)
    return p


def _mix(h, mods, rope, dft, kv_extra, p, *, tm, tq):
    zab, q, k, v, sgu, zp = _inproj(h, mods, rope, p, kv_only=False, tm=tm)
    if len(dft) == 3:
        fre = _dft_fold(zab, *dft)
    else:
        fre = _dft(zab, dft[0], dft[1], tm=min(512, h.shape[1]))
    att = _attention(q, [(k, v)] + kv_extra, tq=tq)
    pool = _pool(zp, p["wp_bd"], p["s_pool"])
    return (fre, att, sgu, pool), (k, v)


def kernel(x, c, ctx, c_ctx, w_ada, b_ada, g_ffn1, w13_ffn1, w2_ffn1, g_mix, w_in, w_fnet, g_q, w_uq, g_kv, w_ukv, g_sgu, w_sgu, b_sgu, w_pool, s_pool, w_out, g_ffn2, w13_ffn2, w2_ffn2, g_final):
    b, l, d = x.shape
    lc = ctx.shape[1]
    depth = w_ada.shape[0]
    tm = 512
    tm_ffn = 1024
    tmc = min(lc, 256)

    cond_rows = 16
    cond = jnp.zeros((cond_rows, d), F32).at[:b].set(c).at[b].set(c_ctx)
    mods_all = _ada(cond, w_ada, b_ada)

    rope = _rope_tables(l)
    rope_c = _identity_rope_tables(lc)
    dft = _folded_token_dft(l)
    dft_c = _token_dft(lc)

    w13_1, w2_1 = w13_ffn1.astype(BF16), w2_ffn1.astype(BF16)
    w13_2, w2_2 = w13_ffn2.astype(BF16), w2_ffn2.astype(BF16)
    w_out_b = w_out.astype(BF16)

    flat = lambda a: a.reshape(1, b * lc, a.shape[-1])
    tmc_ffn = min(b * lc, tm_ffn)

    h, hc = x, ctx
    for i in range(depth):
        last = i == depth - 1
        p = _layer_params(i, g_ffn1, g_mix, w_in, w_fnet, g_q, w_uq, g_kv, w_ukv,
                          g_sgu, w_sgu, b_sgu, w_pool, s_pool, g_ffn2)
        m = mods_all[i, :b].reshape(b, N_MOD, d)
        mc = mods_all[i, b:b + 1].reshape(1, N_MOD, d)

        h = _ffn(h, m, p["g_1"], w13_1, w2_1, i, row0=0, tm=tm_ffn)
        hc = _ffn(flat(hc), mc, p["g_1"], w13_1, w2_1, i, row0=0, tm=tmc_ffn).reshape(b, lc, d)

        if last:
            kc, vc = _inproj(hc, mc, rope_c, p, kv_only=True, tm=tmc)
        else:
            mix_c, (kc, vc) = _mix(hc, mc, rope_c, dft_c, [], p, tm=tmc, tq=tmc)
        mix, _ = _mix(h, m, rope, dft, [(kc, vc)], p, tm=tm, tq=TQ)
        h = _ffn(h, m, p["g_2"], w13_2, w2_2, i, row0=6, tm=tm_ffn, mix=mix + (p["wf_bd"], w_out_b),
                 g_final=g_final.reshape(1, d) if last else None)
        if not last:
            mix_c = tuple(flat(a) for a in mix_c) + (p["wf_bd"], w_out_b)
            hc = _ffn(flat(hc), mc, p["g_2"], w13_2, w2_2, i, row0=6, tm=tmc_ffn, mix=mix_c).reshape(b, lc, d)
    return h
```

```python
import functools
import math

import numpy as np
import jax
import jax.numpy as jnp
from jax import lax
from jax.experimental import pallas as pl
from jax.experimental.pallas import tpu as pltpu

F32 = jnp.float32
BF16 = jnp.bfloat16

D_MODEL = 1024
DEPTH = 2
GRID_W = 64
EPS = 1e-6
N_MOD = 9
GROUP_W = 256
HEADS = 4
HEAD_W = GROUP_W // HEADS
QK_NOPE = 64
QK_ROPE = 32
AXIS_ROPE = QK_ROPE // 2
Q_RANK = 192
KV_RANK = 128
ROPE_BASE = 10000.0
SGU_CHUNK = 128
POOL_WINDOWS = (2, 4, 8, 16)
D_FF = 2816
HEAD_PAD = 128
V_PAD = 96
KEY_CHUNK = 256
SCORE_LOOKAHEAD = 12
FLIP = 256
TABLE_ROWS = 128
ROPE_LANE0 = QK_NOPE

OFF_F = 0
OFF_Q = OFF_F + GROUP_W
OFF_KV = OFF_Q + Q_RANK
OFF_KR = OFF_KV + KV_RANK
OFF_G = OFF_KR + QK_ROPE
OFF_P = OFF_G + 2 * GROUP_W
IN_W = OFF_P + GROUP_W

CAT_F = 0
CAT_G = CAT_F + GROUP_W
CAT_P = CAT_G + 2 * GROUP_W
CAT_Q = CAT_P + GROUP_W
Q_PAD = 256
CAT_KV = CAT_Q + Q_PAD
CAT_KR = CAT_KV + KV_RANK
CAT_W = CAT_KR + HEAD_PAD
ROPE_PAIR = AXIS_ROPE // 2
TQ = 1024
TQ_SUB = 256

MAX_HALF_WINDOW = max(POOL_WINDOWS) // 2
POOL_PAD = 16
VMEM_LIMIT = 56 * 1024 * 1024


def _cparams(*sem):
    return pltpu.CompilerParams(dimension_semantics=sem, vmem_limit_bytes=VMEM_LIMIT)


def _const_spec(shape):
    nd = len(shape)
    return pl.BlockSpec(shape, lambda *_: (0,) * nd, pipeline_mode=pl.Buffered(1))


def _layer_spec(shape, layer):
    nd = len(shape)
    return pl.BlockSpec((1,) + tuple(shape), lambda *_: (layer,) + (0,) * nd, pipeline_mode=pl.Buffered(1))


def _dot(a, b):
    return jnp.dot(a, b, preferred_element_type=F32)


def _rms(x):
    return x * lax.rsqrt(jnp.mean(x * x, axis=-1, keepdims=True) + EPS)


def _norm_mod(x, g, shift, scale):
    return (_rms(x) * g) * (1.0 + scale) + shift


def _ada_kernel(c_ref, w_ref, b_ref, o_ref):
    c = c_ref[...]
    s = c * jax.nn.sigmoid(c)
    hi = s.astype(BF16)
    lo = (s - hi.astype(F32)).astype(BF16)
    r = _dot(jnp.concatenate([hi, lo], axis=0), w_ref[0].astype(BF16))
    rows = c.shape[0]
    o_ref[0] = r[:rows] + r[rows:] + b_ref[0]


def _ada(cond, w_ada, b_ada):
    rows, d = cond.shape
    depth, _, width = w_ada.shape
    bn = 1152
    return pl.pallas_call(
        _ada_kernel,
        out_shape=jax.ShapeDtypeStruct((depth, rows, width), F32),
        grid=(depth, width // bn),
        in_specs=[
            pl.BlockSpec((rows, d), lambda i, j: (0, 0)),
            pl.BlockSpec((1, d, bn), lambda i, j: (i, 0, j)),
            pl.BlockSpec((1, 1, bn), lambda i, j: (i, 0, j)),
        ],
        out_specs=pl.BlockSpec((1, rows, bn), lambda i, j: (i, 0, j)),
        compiler_params=_cparams("parallel", "parallel"),
        name="ada",
    )(cond, w_ada, b_ada.reshape(depth, 1, width))


def _ffn_kernel(h_ref, mod_ref, g_ref, w13_ref, w2_ref, *rest, row0, fchunk, mix_in, final_norm):
    rest = list(rest)
    mix_refs = [rest.pop(0) for _ in range(6)] if mix_in else None
    gf_ref = rest.pop(0) if final_norm else None
    o_ref, xn_ref, act_ref = rest
    shift = mod_ref[0, row0:row0 + 1, :]
    scale = mod_ref[0, row0 + 1:row0 + 2, :]
    gate = mod_ref[0, row0 + 2:row0 + 3, :]
    x = h_ref[0]
    if mix_in:
        f_ref, att_ref, sgu_ref, pool_ref, wf_ref, wo_ref = mix_refs
        fmix = _dot(f_ref[0], wf_ref[...]).astype(BF16)
        mixed = jnp.concatenate([fmix, att_ref[0], sgu_ref[0], pool_ref[0]], axis=1)
        x = x + mod_ref[0, 5:6, :] * _dot(mixed, wo_ref[0])
    xn_ref[...] = _norm_mod(x, g_ref[...], shift, scale).astype(BF16)
    for j in range(D_FF // fchunk):
        sl = slice(j * fchunk, (j + 1) * fchunk)
        a = _dot(xn_ref[...], w13_ref[0, :, sl])
        b = _dot(xn_ref[...], w13_ref[0, :, D_FF + j * fchunk:D_FF + (j + 1) * fchunk])
        act_ref[:, sl] = (a * jax.nn.sigmoid(a) * b).astype(BF16)
    y = _dot(act_ref[...], w2_ref[0])
    out = x + (0.5 * gate) * y
    if final_norm:
        out = _rms(out) * gf_ref[...]
    o_ref[0] = out


def _ffn(h, mods, g, w13, w2, layer, *, row0, tm, mix=None, g_final=None):
    b, l, d = h.shape
    per_batch = mods.shape[0] > 1
    kern = functools.partial(_ffn_kernel, row0=row0, fchunk=256,
                             mix_in=mix is not None, final_norm=g_final is not None)
    extra, extra_specs = [], []
    if mix is not None:
        extra += list(mix)
        extra_specs += [pl.BlockSpec((1, tm, GROUP_W), lambda i, j: (i, j, 0))] * 4
        extra_specs += [_const_spec((GROUP_W, GROUP_W)), _layer_spec((4 * GROUP_W, d), layer)]
    if g_final is not None:
        extra.append(g_final)
        extra_specs.append(_const_spec((1, d)))
    return pl.pallas_call(
        kern,
        out_shape=jax.ShapeDtypeStruct(h.shape, F32),
        grid=(b, l // tm),
        in_specs=[
            pl.BlockSpec((1, tm, d), lambda i, j: (i, j, 0)),
            pl.BlockSpec((1, N_MOD, d), (lambda i, j: (i, 0, 0)) if per_batch else (lambda i, j: (0, 0, 0))),
            _const_spec((1, d)),
            _layer_spec((d, 2 * D_FF), layer),
            _layer_spec((D_FF, d), layer),
        ] + extra_specs,
        out_specs=pl.BlockSpec((1, tm, d), lambda i, j: (i, j, 0)),
        scratch_shapes=[pltpu.VMEM((tm, d), BF16), pltpu.VMEM((tm, D_FF), BF16)],
        compiler_params=_cparams("parallel", "parallel"),
        name="ffn",
    )(h, mods, g, w13, w2, *extra)


def _gelu_tanh(x):
    return 0.5 * x * (1.0 + jnp.tanh(0.7978845608028654 * (x + 0.044715 * (x * x * x))))


def _group_mean(x, ones_bd):
    hi = x.astype(BF16)
    lo = (x - hi.astype(F32)).astype(BF16)
    return _dot(hi, ones_bd) + _dot(lo, ones_bd)


def _rope(x, cos, sin_lo, sin_hi):
    n = x.shape[1]
    up = pltpu.roll(x, n - ROPE_PAIR, axis=1)
    down = pltpu.roll(x, ROPE_PAIR, axis=1)
    return x * cos + up * sin_lo + down * sin_hi


def _inproj_kernel(h_ref, mod_ref, g_ref, cos_ref, slo_ref, shi_ref,
                   wcat_ref, dftc_ref, gq_ref, wuq_ref, gkv_ref, wuk_ref, wuv_ref, vone_ref,
                   gsgu_ref, ones_ref, wsgu_ref, bsgu_ref,
                   *out_refs, kv_only, tm, q_scale):
    if kv_only:
        k_ref, vt_ref = out_refs
    else:
        zab_ref, qt_ref, k_ref, vt_ref, sgu_ref, zp_ref = out_refs
    shift = mod_ref[0, 3:4, :]
    scale = mod_ref[0, 4:5, :]
    n = _norm_mod(h_ref[0], g_ref[...], shift, scale).astype(BF16)
    rope = (cos_ref[...], slo_ref[...], shi_ref[...])

    def proj(c0, c1):
        return _dot(n, wcat_ref[:, c0:c1])

    zkv = proj(CAT_KV, CAT_W)
    if not kv_only:
        zq = proj(CAT_Q, CAT_KV)

    kvn = (_rms(zkv[:, :KV_RANK]) * gkv_ref[...]).astype(BF16)
    k = _dot(kvn, wuk_ref[...]) + jnp.concatenate([_rope(zkv[:, KV_RANK:], *rope)] * HEADS, axis=1)
    k_ref[0] = k.astype(BF16)
    vt_ref[0] = (_dot(kvn, wuv_ref[...]) + vone_ref[...]).T.astype(BF16)
    if kv_only:
        return

    zg = proj(CAT_G, CAT_P)

    ms = jnp.sum(zq * zq, axis=-1, keepdims=True) * (1.0 / Q_RANK)
    qn = (zq * lax.rsqrt(ms + EPS) * gq_ref[...]).astype(BF16)
    rope4 = [jnp.concatenate([t] * HEADS, axis=1) for t in rope]
    q = _rope(_dot(qn, wuq_ref[...]), *rope4)
    qt_ref[0] = (q * q_scale).T.astype(BF16)

    zf = proj(CAT_F, CAT_G)
    zp_ref[0] = proj(CAT_P, CAT_Q)

    zab_ref[0] = _dot(zf.astype(BF16), dftc_ref[...]).astype(BF16)

    gz = _gelu_tanh(zg)
    u = gz[:, :GROUP_W]
    vv = gz[:, GROUP_W:]
    ms = _group_mean(vv * vv, ones_ref[...])
    vn = (vv * lax.rsqrt(ms + EPS) * gsgu_ref[...]).astype(BF16)
    lane = lax.broadcasted_iota(jnp.int32, (SGU_CHUNK, GROUP_W), 1)
    even_head = (lane // HEAD_W) % 2 == 0
    zero = jnp.zeros((SGU_CHUNK, GROUP_W), BF16)
    for c in range(tm // SGU_CHUNK):
        rows = slice(c * SGU_CHUNK, (c + 1) * SGU_CHUNK)
        stacked = jnp.concatenate([jnp.where(even_head, vn[rows], zero), jnp.where(even_head, zero, vn[rows])], axis=0)
        r = _dot(wsgu_ref[...], stacked)
        sel = jnp.where(lane < 2 * HEAD_W, r[:SGU_CHUNK], r[SGU_CHUNK:])
        sgu_ref[0, rows, :] = (u[rows] * (sel + bsgu_ref[...])).astype(BF16)


def _inproj(h, mods, rope, p, *, kv_only, tm):
    b, l, d = h.shape
    per_batch = mods.shape[0] > 1
    weights = [p["w_cat"], p["dft_c"], p["g_q"], p["w_uq"], p["g_kv"], p["w_uk"], p["w_uv"], p["v_one"],
               p["g_sgu"], p["ones_bd"], p["w_sgu"], p["b_sgu"]]
    rope_spec = pl.BlockSpec((tm, HEAD_PAD), lambda i, j: (j, 0))
    tok = lambda w, dt: jax.ShapeDtypeStruct((b, l, w), dt)
    tok_spec = lambda w: pl.BlockSpec((1, tm, w), lambda i, j: (i, j, 0))
    tr = lambda w: jax.ShapeDtypeStruct((b, w, l), BF16)
    tr_spec = lambda w: pl.BlockSpec((1, w, tm), lambda i, j: (i, 0, j))
    if kv_only:
        out_shape = [tok(HEADS * HEAD_PAD, BF16), tr(HEADS * V_PAD)]
        out_specs = [tok_spec(HEADS * HEAD_PAD), tr_spec(HEADS * V_PAD)]
    else:
        out_shape = [tok(2 * GROUP_W, BF16), tr(HEADS * HEAD_PAD), tok(HEADS * HEAD_PAD, BF16),
                     tr(HEADS * V_PAD), tok(GROUP_W, BF16), tok(GROUP_W, F32)]
        out_specs = [tok_spec(2 * GROUP_W), tr_spec(HEADS * HEAD_PAD), tok_spec(HEADS * HEAD_PAD),
                     tr_spec(HEADS * V_PAD), tok_spec(GROUP_W), tok_spec(GROUP_W)]
    kern = functools.partial(_inproj_kernel, kv_only=kv_only, tm=tm,
                             q_scale=float(QK_NOPE + QK_ROPE) ** -0.5 * math.log2(math.e))
    return pl.pallas_call(
        kern,
        out_shape=out_shape,
        grid=(b, l // tm),
        in_specs=[
            pl.BlockSpec((1, tm, d), lambda i, j: (i, j, 0)),
            pl.BlockSpec((1, N_MOD, d), (lambda i, j: (i, 0, 0)) if per_batch else (lambda i, j: (0, 0, 0))),
            _const_spec((1, d)),
            rope_spec, rope_spec, rope_spec,
        ] + [_const_spec(w.shape) for w in weights],
        out_specs=out_specs,
        compiler_params=_cparams("parallel", "parallel"),
        name="inproj",
    )(h, mods, p["g_mix"], *rope, *weights)


def _dft_kernel(c_ref, s_ref, z_ref, o_ref):
    za = z_ref[0, :, :GROUP_W]
    zb = z_ref[0, :, GROUP_W:]
    o_ref[0] = (_dot(c_ref[...], za) - _dot(s_ref[...], zb)).astype(BF16)


def _dft(zab, cmat, smat, *, tm):
    b, l, _ = zab.shape
    return pl.pallas_call(
        _dft_kernel,
        out_shape=jax.ShapeDtypeStruct((b, l, GROUP_W), BF16),
        grid=(l // tm, b),
        in_specs=[
            pl.BlockSpec((tm, l), lambda i, j: (i, 0)),
            pl.BlockSpec((tm, l), lambda i, j: (i, 0)),
            pl.BlockSpec((1, l, 2 * GROUP_W), lambda i, j: (j, 0, 0)),
        ],
        out_specs=pl.BlockSpec((1, tm, GROUP_W), lambda i, j: (j, i, 0)),
        compiler_params=_cparams("parallel", "parallel"),
        name="dft",
    )(cmat, smat, zab)


def _dft_fold_kernel(cq_ref, sq_ref, rev_ref, z_ref, o_ref, fold_ref, t_ref, *, l):
    half = l // 2
    nb = half // FLIP
    rev = rev_ref[...]
    fold_ref[0:8, :] = jnp.zeros((8, 2 * GROUP_W), F32)
    for jb in range(nb):
        blk = z_ref[0, (2 * nb - 1 - jb) * FLIP:(2 * nb - jb) * FLIP, :]
        fold_ref[8 + jb * FLIP:8 + (jb + 1) * FLIP, :] = _dot(rev, blk)
    zlo = z_ref[0, 0:half, :].astype(F32)
    mirrored = fold_ref[7:7 + half, :]
    even = (zlo[:, :GROUP_W] + mirrored[:, :GROUP_W]).astype(BF16)
    odd = (zlo[:, GROUP_W:] - mirrored[:, GROUP_W:]).astype(BF16)
    p = _dot(cq_ref[...], even)
    q = _dot(sq_ref[...], odd)
    k_idx = lax.broadcasted_iota(jnp.int32, (p.shape[0], 1), 0)
    sign = (1 - 2 * (k_idx & 1)).astype(F32) * (1.0 / math.sqrt(l))
    p = p + sign * z_ref[0, half:half + 1, :GROUP_W].astype(F32)
    o_ref[0, 0:half, :] = (p[:half] - q[:half]).astype(BF16)
    t_ref[...] = p + q
    upper = t_ref[1:half + 1, :].astype(BF16)
    for jb in range(nb):
        blk = upper[(nb - 1 - jb) * FLIP:(nb - jb) * FLIP, :]
        o_ref[0, half + jb * FLIP:half + (jb + 1) * FLIP, :] = _dot(rev, blk).astype(BF16)


def _dft_fold(zab, cq, sq, rev):
    b, l, _ = zab.shape
    half = l // 2
    rows = half + 16
    return pl.pallas_call(
        functools.partial(_dft_fold_kernel, l=l),
        out_shape=jax.ShapeDtypeStruct((b, l, GROUP_W), BF16),
        grid=(b,),
        in_specs=[
            _const_spec((rows, half)),
            _const_spec((rows, half)),
            _const_spec((FLIP, FLIP)),
            pl.BlockSpec((1, l, 2 * GROUP_W), lambda i: (i, 0, 0)),
        ],
        out_specs=pl.BlockSpec((1, l, GROUP_W), lambda i: (i, 0, 0)),
        scratch_shapes=[pltpu.VMEM((half + 8, 2 * GROUP_W), F32), pltpu.VMEM((rows, GROUP_W), F32)],
        compiler_params=_cparams("parallel"),
        name="dft_fold",
    )(cq, sq, rev, zab)


def _attn_kernel(qt_ref, *refs, seg_lens):
    n_seg = len(seg_lens)
    kv_refs = refs[:2 * n_seg]
    o_ref = refs[2 * n_seg]
    acc_ref = refs[2 * n_seg + 1]
    tq = qt_ref.shape[2]
    chains = [(qs, hd) for qs in range(0, tq, TQ_SUB) for hd in range(HEADS)]
    items = []
    for s, lk in enumerate(seg_lens):
        chunk = min(KEY_CHUNK, lk)
        for c0 in range(0, lk, chunk):
            items += [(s, c0, chunk, ch) for ch in range(len(chains))]
    m = [None] * len(chains)
    acc = [None] * len(chains)
    scores = {}
    for i in range(len(items) + SCORE_LOOKAHEAD):
        if i < len(items):
            s, c0, chunk, ch = items[i]
            qs, hd = chains[ch]
            k = kv_refs[2 * s][0, c0:c0 + chunk, hd * HEAD_PAD:(hd + 1) * HEAD_PAD]
            scores[i] = _dot(k, qt_ref[0, hd * HEAD_PAD:(hd + 1) * HEAD_PAD, qs:qs + TQ_SUB])
        j = i - SCORE_LOOKAHEAD
        if j < 0:
            continue
        s, c0, chunk, ch = items[j]
        qs, hd = chains[ch]
        sc = scores.pop(j)
        cm = sc.max(axis=0, keepdims=True)
        m_new = cm if m[ch] is None else jnp.maximum(m[ch], cm)
        p = jnp.exp2(sc - m_new).astype(BF16)
        vt = kv_refs[2 * s + 1][0, hd * V_PAD:(hd + 1) * V_PAD, c0:c0 + chunk]
        pv = _dot(vt, p)
        acc[ch] = pv if acc[ch] is None else acc[ch] * jnp.exp2(m[ch] - m_new) + pv
        m[ch] = m_new
    for ch, (qs, hd) in enumerate(chains):
        inv = 1.0 / acc[ch][HEAD_W:HEAD_W + 1]
        acc_ref[hd * HEAD_W:(hd + 1) * HEAD_W, qs:qs + TQ_SUB] = acc[ch][:HEAD_W] * inv
    o_ref[0] = acc_ref[...].T.astype(BF16)


def _attention(qt, segments, *, tq):
    b, _, l = qt.shape
    in_specs = [pl.BlockSpec((1, HEADS * HEAD_PAD, tq), lambda i, j: (i, 0, j))]
    args = [qt]
    for k, vt in segments:
        lk = k.shape[1]
        in_specs.append(pl.BlockSpec((1, lk, HEADS * HEAD_PAD), lambda i, j: (i, 0, 0)))
        in_specs.append(pl.BlockSpec((1, HEADS * V_PAD, lk), lambda i, j: (i, 0, 0)))
        args += [k, vt]
    return pl.pallas_call(
        functools.partial(_attn_kernel, seg_lens=tuple(k.shape[1] for k, _ in segments)),
        out_shape=jax.ShapeDtypeStruct((b, l, GROUP_W), BF16),
        grid=(b, l // tq),
        in_specs=in_specs,
        out_specs=pl.BlockSpec((1, tq, GROUP_W), lambda i, j: (i, j, 0)),
        scratch_shapes=[pltpu.VMEM((GROUP_W, tq), F32)],
        compiler_params=_cparams("parallel", "parallel"),
        name="attention",
    )(*args)


def _pool_kernel(z_ref, wp_ref, sp_ref, o_ref, pad_ref, *, l, rb):
    zeros = jnp.zeros((POOL_PAD, GROUP_W), F32)
    pad_ref[0:POOL_PAD, :] = zeros
    pad_ref[POOL_PAD + l:2 * POOL_PAD + l, :] = zeros
    pad_ref[POOL_PAD:POOL_PAD + l, :] = z_ref[0]
    halo = MAX_HALF_WINDOW
    n = rb + 2 * halo
    lane = lax.broadcasted_iota(jnp.int32, (rb, HEAD_PAD), 1)
    first_group = lane < HEAD_W

    def up(a, kk):
        return pltpu.roll(a, n - kk, axis=0)

    def down(a, kk):
        return pltpu.roll(a, kk, axis=0)

    for r in range(l // rb):
        base = POOL_PAD + r * rb - halo
        edge = r == 0 or r == l // rb - 1
        pooled = []
        for tile in range(GROUP_W // HEAD_PAD):
            lanes = slice(tile * HEAD_PAD, (tile + 1) * HEAD_PAD)
            w_a, w_b = POOL_WINDOWS[2 * tile], POOL_WINDOWS[2 * tile + 1]
            x = pad_ref[base:base + n, lanes]
            run = {1: x}
            width = 1
            while width < min(w_b, halo):
                run[2 * width] = run[width] + up(run[width], width)
                width *= 2

            def window(w):
                if w in run:
                    return down(run[w], w // 2)
                return down(run[w // 2], w // 2) + run[w // 2]

            win = jnp.where(first_group, window(w_a)[halo:halo + rb], window(w_b)[halo:halo + rb])
            z = x[halo:halo + rb]
            if edge:
                t = lax.broadcasted_iota(jnp.int32, (rb, HEAD_PAD), 0) + r * rb
                hw = jnp.where(first_group, w_a // 2, w_b // 2)
                cnt = jnp.minimum(t + hw, l) - jnp.maximum(t - hw, 0)
                pooled.append(win / cnt.astype(F32) - z)
            else:
                pooled.append(win * jnp.where(first_group, 1.0 / w_a, 1.0 / w_b) - z)
        y = _dot(jnp.concatenate(pooled, axis=1).astype(BF16), wp_ref[...]) * sp_ref[...]
        o_ref[0, r * rb:(r + 1) * rb, :] = y.astype(BF16)


def _pool(zp, wp_bd, s_pool):
    b, l, _ = zp.shape
    rb = min(l, 256)
    return pl.pallas_call(
        functools.partial(_pool_kernel, l=l, rb=rb),
        out_shape=jax.ShapeDtypeStruct((b, l, GROUP_W), BF16),
        grid=(b,),
        in_specs=[
            pl.BlockSpec((1, l, GROUP_W), lambda i: (i, 0, 0)),
            _const_spec((GROUP_W, GROUP_W)),
            _const_spec((1, GROUP_W)),
        ],
        out_specs=pl.BlockSpec((1, l, GROUP_W), lambda i: (i, 0, 0)),
        scratch_shapes=[pltpu.VMEM((l + 2 * POOL_PAD, GROUP_W), F32)],
        compiler_params=_cparams("parallel"),
        name="pool",
    )(zp, wp_bd, s_pool)


def _block_diag(blocks):
    n, r, c = blocks.shape
    out = jnp.zeros((n * r, n * c), blocks.dtype)
    for i in range(n):
        out = lax.dynamic_update_slice(out, blocks[i], (i * r, i * c))
    return out


def _channel_dft():
    k = np.arange(HEAD_W)
    ang = 2.0 * np.pi * np.outer(k, k) / HEAD_W
    eye = np.eye(HEADS)
    c = np.kron(eye, np.cos(ang)) / math.sqrt(HEAD_W)
    s = np.kron(eye, np.sin(ang)) / math.sqrt(HEAD_W)
    return jnp.asarray(np.concatenate([c, s], axis=1), F32).astype(BF16)


def _token_dft(l, rows=None, cols=None):
    kk = jnp.arange(l if rows is None else rows, dtype=jnp.int32)
    tt = jnp.arange(l if cols is None else cols, dtype=jnp.int32)
    ang = ((kk[:, None] * tt[None, :]) % l).astype(F32) * (2.0 * math.pi / l)
    norm = 1.0 / math.sqrt(l)
    return (jnp.cos(ang) * norm).astype(BF16), (jnp.sin(ang) * norm).astype(BF16)


def _dft_table_kernel(ca_ref, sa_ref, cb_ref, sb_ref, c_ref, s_ref):
    ca, sa = ca_ref[0], sa_ref[0]
    cb, sb = cb_ref[...], sb_ref[...]
    c_ref[...] = (ca * cb - sa * sb).astype(BF16)
    s_ref[...] = (sa * cb + ca * sb).astype(BF16)


def _folded_token_dft(l):
    half = l // 2
    blocks = pl.cdiv(half + 1, TABLE_ROWS)
    tt = jnp.arange(half, dtype=jnp.int32)[None, :]
    angle = lambda kk: ((kk[:, None] * tt) % l).astype(F32) * (2.0 * math.pi / l)
    coarse = angle(jnp.arange(blocks, dtype=jnp.int32) * TABLE_ROWS)
    fine = angle(jnp.arange(TABLE_ROWS, dtype=jnp.int32))
    norm = 1.0 / math.sqrt(l)
    row_spec = pl.BlockSpec((1, 1, half), lambda i: (i, 0, 0))
    out_spec = pl.BlockSpec((TABLE_ROWS, half), lambda i: (i, 0))
    table = jax.ShapeDtypeStruct((blocks * TABLE_ROWS, half), BF16)
    cq, sq = pl.pallas_call(
        _dft_table_kernel,
        out_shape=[table, table],
        grid=(blocks,),
        in_specs=[row_spec, row_spec, _const_spec((TABLE_ROWS, half)), _const_spec((TABLE_ROWS, half))],
        out_specs=[out_spec, out_spec],
        compiler_params=_cparams("parallel"),
        name="dft_tables",
    )(jnp.cos(coarse)[:, None, :], jnp.sin(coarse)[:, None, :], jnp.cos(fine) * norm, jnp.sin(fine) * norm)
    rev = jnp.asarray(np.eye(FLIP)[::-1].copy(), F32).astype(BF16)
    return cq, sq, rev


def _rope_tables(l):
    pos = jnp.arange(l)
    row = (pos // GRID_W).astype(F32)
    col = (pos % GRID_W).astype(F32)
    inv = jnp.power(ROPE_BASE, -jnp.arange(0, AXIS_ROPE, 2, dtype=F32) / AXIS_ROPE)
    zero = jnp.zeros((l, ROPE_PAIR), F32)
    row_sin, col_sin = jnp.sin(row[:, None] * inv), jnp.sin(col[:, None] * inv)
    row_cos, col_cos = jnp.cos(row[:, None] * inv), jnp.cos(col[:, None] * inv)
    place = lambda base, parts: base.at[:, ROPE_LANE0:ROPE_LANE0 + QK_ROPE].set(jnp.concatenate(parts, axis=1))
    cos = place(jnp.ones((l, HEAD_PAD), F32), [row_cos, row_cos, col_cos, col_cos])
    sin_lo = place(jnp.zeros((l, HEAD_PAD), F32), [-row_sin, zero, -col_sin, zero])
    sin_hi = place(jnp.zeros((l, HEAD_PAD), F32), [zero, row_sin, zero, col_sin])
    return cos, sin_lo, sin_hi


def _identity_rope_tables(l):
    zero = jnp.zeros((l, HEAD_PAD), F32)
    return jnp.ones((l, HEAD_PAD), F32), zero, zero


def _layer_params(i, g_ffn1, g_mix, w_in, w_fnet, g_q, w_uq, g_kv, w_ukv,
                  g_sgu, w_sgu, b_sgu, w_pool, s_pool, g_ffn2):
    p = {}
    p["g_1"] = g_ffn1[i].reshape(1, D_MODEL)
    p["g_2"] = g_ffn2[i].reshape(1, D_MODEL)
    wi = w_in[i]
    p["g_mix"] = g_mix[i].reshape(1, D_MODEL)
    zcols = lambda n: jnp.zeros((D_MODEL, n), F32)
    p["w_cat"] = jnp.concatenate([
        wi[:, OFF_F:OFF_Q], wi[:, OFF_G:OFF_P], wi[:, OFF_P:],
        wi[:, OFF_Q:OFF_KV], zcols(Q_PAD - Q_RANK), wi[:, OFF_KV:OFF_KR],
        zcols(ROPE_LANE0), wi[:, OFF_KR:OFF_G], zcols(HEAD_PAD - ROPE_LANE0 - QK_ROPE)], axis=1).astype(BF16)
    p["dft_c"] = _channel_dft()
    p["g_q"] = jnp.concatenate([g_q[i], jnp.zeros((Q_PAD - Q_RANK,), F32)]).reshape(1, Q_PAD)
    wuq = w_uq[i].reshape(Q_RANK, HEADS, QK_NOPE + QK_ROPE)
    wuq = jnp.concatenate([wuq, jnp.zeros((Q_RANK, HEADS, HEAD_PAD - QK_NOPE - QK_ROPE), F32)], axis=-1)
    p["w_uq"] = jnp.concatenate([wuq.reshape(Q_RANK, HEADS * HEAD_PAD),
                                 jnp.zeros((Q_PAD - Q_RANK, HEADS * HEAD_PAD), F32)], axis=0).astype(BF16)
    p["g_kv"] = g_kv[i].reshape(1, KV_RANK)
    wukv = w_ukv[i].reshape(KV_RANK, HEADS, QK_NOPE + HEAD_W)
    p["w_uk"] = jnp.concatenate([wukv[..., :QK_NOPE], jnp.zeros((KV_RANK, HEADS, HEAD_PAD - QK_NOPE), F32)],
                                axis=-1).reshape(KV_RANK, HEADS * HEAD_PAD).astype(BF16)
    p["w_uv"] = jnp.concatenate([wukv[..., QK_NOPE:], jnp.zeros((KV_RANK, HEADS, V_PAD - HEAD_W), F32)],
                                axis=-1).reshape(KV_RANK, HEADS * V_PAD).astype(BF16)
    p["v_one"] = jnp.zeros((1, HEADS, V_PAD), F32).at[:, :, HEAD_W].set(1.0).reshape(1, HEADS * V_PAD)
    p["g_sgu"] = g_sgu[i].reshape(1, GROUP_W)
    p["ones_bd"] = jnp.asarray(np.kron(np.eye(HEADS), np.full((HEAD_W, HEAD_W), 1.0 / HEAD_W)), BF16)
    ws = w_sgu[i]
    p["w_sgu"] = jnp.concatenate([jnp.concatenate([ws[0], ws[1]], axis=1),
                                  jnp.concatenate([ws[2], ws[3]], axis=1)], axis=0).astype(BF16)
    p["b_sgu"] = jnp.repeat(b_sgu[i].T, HEAD_W, axis=1)
    p["wf_bd"] = _block_diag(w_fnet[i]).astype(BF16)
    p["wp_bd"] = _block_diag(w_pool[i]).astype(BF16)
    p["s_pool"] = s_pool[i].reshape(1, GROUP_W)
    return p


def _mix(h, mods, rope, dft, kv_extra, p, *, tm, tq):
    zab, q, k, v, sgu, zp = _inproj(h, mods, rope, p, kv_only=False, tm=tm)
    if len(dft) == 3:
        fre = _dft_fold(zab, *dft)
    else:
        fre = _dft(zab, dft[0], dft[1], tm=min(512, h.shape[1]))
    att = _attention(q, [(k, v)] + kv_extra, tq=tq)
    pool = _pool(zp, p["wp_bd"], p["s_pool"])
    return (fre, att, sgu, pool), (k, v)


def kernel(x, c, ctx, c_ctx, w_ada, b_ada, g_ffn1, w13_ffn1, w2_ffn1, g_mix, w_in, w_fnet, g_q, w_uq, g_kv, w_ukv, g_sgu, w_sgu, b_sgu, w_pool, s_pool, w_out, g_ffn2, w13_ffn2, w2_ffn2, g_final):
    b, l, d = x.shape
    lc = ctx.shape[1]
    depth = w_ada.shape[0]
    tm = 1024
    tm_ffn = 1024
    tmc = min(lc, 256)

    cond_rows = 16
    cond = jnp.zeros((cond_rows, d), F32).at[:b].set(c).at[b].set(c_ctx)
    mods_all = _ada(cond, w_ada, b_ada)

    rope = _rope_tables(l)
    rope_c = _identity_rope_tables(lc)
    dft = _folded_token_dft(l)
    dft_c = _token_dft(lc)

    w13_1, w2_1 = w13_ffn1.astype(BF16), w2_ffn1.astype(BF16)
    w13_2, w2_2 = w13_ffn2.astype(BF16), w2_ffn2.astype(BF16)
    w_out_b = w_out.astype(BF16)

    flat = lambda a: a.reshape(1, b * lc, a.shape[-1])
    tmc_ffn = min(b * lc, tm_ffn)

    h, hc = x, ctx
    for i in range(depth):
        last = i == depth - 1
        p = _layer_params(i, g_ffn1, g_mix, w_in, w_fnet, g_q, w_uq, g_kv, w_ukv,
                          g_sgu, w_sgu, b_sgu, w_pool, s_pool, g_ffn2)
        m = mods_all[i, :b].reshape(b, N_MOD, d)
        mc = mods_all[i, b:b + 1].reshape(1, N_MOD, d)

        h = _ffn(h, m, p["g_1"], w13_1, w2_1, i, row0=0, tm=tm_ffn)
        hc = _ffn(flat(hc), mc, p["g_1"], w13_1, w2_1, i, row0=0, tm=tmc_ffn).reshape(b, lc, d)

        if last:
            kc, vc = _inproj(hc, mc, rope_c, p, kv_only=True, tm=tmc)
        else:
            mix_c, (kc, vc) = _mix(hc, mc, rope_c, dft_c, [], p, tm=tmc, tq=tmc)
        mix, _ = _mix(h, m, rope, dft, [(kc, vc)], p, tm=tm, tq=TQ)
        h = _ffn(h, m, p["g_2"], w13_2, w2_2, i, row0=6, tm=tm_ffn, mix=mix + (p["wf_bd"], w_out_b),
                 g_final=g_final.reshape(1, d) if last else None)
        if not last:
            mix_c = tuple(flat(a) for a in mix_c) + (p["wf_bd"], w_out_b)
            hc = _ffn(flat(hc), mc, p["g_2"], w13_2, w2_2, i, row0=6, tm=tmc_ffn, mix=mix_c).reshape(b, lc, d)
    return h
```

```python
import functools
import math

import numpy as np
import jax
import jax.numpy as jnp
from jax import lax
from jax.experimental import pallas as pl
from jax.experimental.pallas import tpu as pltpu

F32 = jnp.float32
BF16 = jnp.bfloat16

D_MODEL = 1024
DEPTH = 2
GRID_W = 64
EPS = 1e-6
N_MOD = 9
GROUP_W = 256
HEADS = 4
HEAD_W = GROUP_W // HEADS
QK_NOPE = 64
QK_ROPE = 32
AXIS_ROPE = QK_ROPE // 2
Q_RANK = 192
KV_RANK = 128
ROPE_BASE = 10000.0
SGU_CHUNK = 128
POOL_WINDOWS = (2, 4, 8, 16)
D_FF = 2816
HEAD_PAD = 128
V_PAD = 96
KEY_CHUNK = 256
SCORE_LOOKAHEAD = 12
FLIP = 256
TABLE_ROWS = 256
ROPE_LANE0 = QK_NOPE

OFF_F = 0
OFF_Q = OFF_F + GROUP_W
OFF_KV = OFF_Q + Q_RANK
OFF_KR = OFF_KV + KV_RANK
OFF_G = OFF_KR + QK_ROPE
OFF_P = OFF_G + 2 * GROUP_W
IN_W = OFF_P + GROUP_W

CAT_F = 0
CAT_G = CAT_F + GROUP_W
CAT_P = CAT_G + 2 * GROUP_W
CAT_Q = CAT_P + GROUP_W
Q_PAD = 256
CAT_KV = CAT_Q + Q_PAD
CAT_KR = CAT_KV + KV_RANK
CAT_W = CAT_KR + HEAD_PAD
ROPE_PAIR = AXIS_ROPE // 2
TQ = 1024
TQ_SUB = 256

MAX_HALF_WINDOW = max(POOL_WINDOWS) // 2
POOL_PAD = 16
VMEM_LIMIT = 56 * 1024 * 1024


def _cparams(*sem):
    return pltpu.CompilerParams(dimension_semantics=sem, vmem_limit_bytes=VMEM_LIMIT)


def _const_spec(shape):
    nd = len(shape)
    return pl.BlockSpec(shape, lambda *_: (0,) * nd, pipeline_mode=pl.Buffered(1))


def _layer_spec(shape, layer):
    nd = len(shape)
    return pl.BlockSpec((1,) + tuple(shape), lambda *_: (layer,) + (0,) * nd, pipeline_mode=pl.Buffered(1))


def _mod_spec(mods, mod_row):
    base, per_batch = mod_row
    return pl.BlockSpec((1,) + mods.shape[1:], lambda i, j: (base + i if per_batch else base, 0, 0))


def _dot(a, b):
    return jnp.dot(a, b, preferred_element_type=F32)


def _rms(x):
    return x * lax.rsqrt(jnp.mean(x * x, axis=-1, keepdims=True) + EPS)


def _norm_mod(x, g, shift, scale):
    return (_rms(x) * g) * (1.0 + scale) + shift


def _ada_kernel(c_ref, w_ref, b_ref, o_ref):
    c = c_ref[...]
    s = c * jax.nn.sigmoid(c)
    hi = s.astype(BF16)
    lo = (s - hi.astype(F32)).astype(BF16)
    r = _dot(jnp.concatenate([hi, lo], axis=0), w_ref[0].astype(BF16))
    rows = c.shape[0]
    o_ref[0] = r[:rows] + r[rows:] + b_ref[0]


def _ada(cond, w_ada, b_ada):
    rows, d = cond.shape
    depth, _, width = w_ada.shape
    bn = 2304
    return pl.pallas_call(
        _ada_kernel,
        out_shape=jax.ShapeDtypeStruct((depth, rows, width), F32),
        grid=(depth, width // bn),
        in_specs=[
            pl.BlockSpec((rows, d), lambda i, j: (0, 0)),
            pl.BlockSpec((1, d, bn), lambda i, j: (i, 0, j)),
            pl.BlockSpec((1, 1, bn), lambda i, j: (i, 0, j)),
        ],
        out_specs=pl.BlockSpec((1, rows, bn), lambda i, j: (i, 0, j)),
        compiler_params=_cparams("parallel", "parallel"),
        name="ada",
    )(cond, w_ada, b_ada.reshape(depth, 1, width))


def _ffn_kernel(h_ref, mod_ref, g_ref, w13_ref, w2_ref, *rest, row0, fchunk, mix_in, final_norm):
    rest = list(rest)
    mix_refs = [rest.pop(0) for _ in range(6)] if mix_in else None
    gf_ref = rest.pop(0) if final_norm else None
    o_ref, xn_ref, act_ref = rest
    shift = mod_ref[0, row0:row0 + 1, :]
    scale = mod_ref[0, row0 + 1:row0 + 2, :]
    gate = mod_ref[0, row0 + 2:row0 + 3, :]
    x = h_ref[0]
    if mix_in:
        f_ref, att_ref, sgu_ref, pool_ref, wf_ref, wo_ref = mix_refs
        fmix = _dot(f_ref[0], wf_ref[0]).astype(BF16)
        mixed = jnp.concatenate([fmix, att_ref[0], sgu_ref[0], pool_ref[0]], axis=1)
        x = x + mod_ref[0, 5:6, :] * _dot(mixed, wo_ref[0])
    xn_ref[...] = _norm_mod(x, g_ref[0], shift, scale).astype(BF16)
    for j in range(D_FF // fchunk):
        sl = slice(j * fchunk, (j + 1) * fchunk)
        a = _dot(xn_ref[...], w13_ref[0, :, sl])
        b = _dot(xn_ref[...], w13_ref[0, :, D_FF + j * fchunk:D_FF + (j + 1) * fchunk])
        act_ref[:, sl] = (a * jax.nn.sigmoid(a) * b).astype(BF16)
    y = _dot(act_ref[...], w2_ref[0])
    out = x + (0.5 * gate) * y
    if final_norm:
        out = _rms(out) * gf_ref[...]
    o_ref[0] = out


def _ffn(h, mods, mod_row, g, w13, w2, layer, *, row0, tm, mix=None, g_final=None):
    b, l, d = h.shape
    kern = functools.partial(_ffn_kernel, row0=row0, fchunk=256,
                             mix_in=mix is not None, final_norm=g_final is not None)
    extra, extra_specs = [], []
    if mix is not None:
        extra += list(mix)
        extra_specs += [pl.BlockSpec((1, tm, GROUP_W), lambda i, j: (i, j, 0))] * 4
        extra_specs += [_layer_spec((GROUP_W, GROUP_W), layer), _layer_spec((4 * GROUP_W, d), layer)]
    if g_final is not None:
        extra.append(g_final)
        extra_specs.append(_const_spec((1, d)))
    return pl.pallas_call(
        kern,
        out_shape=jax.ShapeDtypeStruct(h.shape, F32),
        grid=(b, l // tm),
        in_specs=[
            pl.BlockSpec((1, tm, d), lambda i, j: (i, j, 0)),
            _mod_spec(mods, mod_row),
            _layer_spec((1, d), layer),
            _layer_spec((d, 2 * D_FF), layer),
            _layer_spec((D_FF, d), layer),
        ] + extra_specs,
        out_specs=pl.BlockSpec((1, tm, d), lambda i, j: (i, j, 0)),
        scratch_shapes=[pltpu.VMEM((tm, d), BF16), pltpu.VMEM((tm, D_FF), BF16)],
        compiler_params=_cparams("parallel", "parallel"),
        name="ffn",
    )(h, mods, g, w13, w2, *extra)


def _gelu_tanh(x):
    return 0.5 * x * (1.0 + jnp.tanh(0.7978845608028654 * (x + 0.044715 * (x * x * x))))


def _group_mean(x, ones_bd):
    hi = x.astype(BF16)
    lo = (x - hi.astype(F32)).astype(BF16)
    return _dot(hi, ones_bd) + _dot(lo, ones_bd)


def _rope(x, cos, sin_lo, sin_hi):
    n = x.shape[1]
    up = pltpu.roll(x, n - ROPE_PAIR, axis=1)
    down = pltpu.roll(x, ROPE_PAIR, axis=1)
    return x * cos + up * sin_lo + down * sin_hi


def _inproj_kernel(h_ref, mod_ref, g_ref, cos_ref, slo_ref, shi_ref,
                   wcat_ref, dftc_ref, gq_ref, wuq_ref, gkv_ref, wuk_ref, wuv_ref, vone_ref,
                   gsgu_ref, ones_ref, wsgu_ref, bsgu_ref,
                   *out_refs, kv_only, tm, q_scale):
    if kv_only:
        k_ref, vt_ref = out_refs
    else:
        zab_ref, qt_ref, k_ref, vt_ref, sgu_ref, zp_ref = out_refs
    shift = mod_ref[0, 3:4, :]
    scale = mod_ref[0, 4:5, :]
    n = _norm_mod(h_ref[0], g_ref[0], shift, scale).astype(BF16)
    rope = (cos_ref[...], slo_ref[...], shi_ref[...])

    def proj(c0, c1):
        return _dot(n, wcat_ref[0, :, c0:c1])

    zkv = proj(CAT_KV, CAT_W)
    if not kv_only:
        zq = proj(CAT_Q, CAT_KV)

    kvn = (_rms(zkv[:, :KV_RANK]) * gkv_ref[0]).astype(BF16)
    k = _dot(kvn, wuk_ref[0]) + jnp.concatenate([_rope(zkv[:, KV_RANK:], *rope)] * HEADS, axis=1)
    k_ref[0] = k.astype(BF16)
    vt_ref[0] = (_dot(kvn, wuv_ref[0]) + vone_ref[...]).T.astype(BF16)
    if kv_only:
        return

    zg = proj(CAT_G, CAT_P)

    ms = jnp.sum(zq * zq, axis=-1, keepdims=True) * (1.0 / Q_RANK)
    qn = (zq * lax.rsqrt(ms + EPS) * gq_ref[0]).astype(BF16)
    rope4 = [jnp.concatenate([t] * HEADS, axis=1) for t in rope]
    q = _rope(_dot(qn, wuq_ref[0]), *rope4)
    qt_ref[0] = (q * q_scale).T.astype(BF16)

    zf = proj(CAT_F, CAT_G)
    zp_ref[0] = proj(CAT_P, CAT_Q)

    zab_ref[0] = _dot(zf.astype(BF16), dftc_ref[...]).astype(BF16)

    gz = _gelu_tanh(zg)
    u = gz[:, :GROUP_W]
    vv = gz[:, GROUP_W:]
    ms = _group_mean(vv * vv, ones_ref[...])
    vn = (vv * lax.rsqrt(ms + EPS) * gsgu_ref[0]).astype(BF16)
    lane = lax.broadcasted_iota(jnp.int32, (SGU_CHUNK, GROUP_W), 1)
    even_head = (lane // HEAD_W) % 2 == 0
    zero = jnp.zeros((SGU_CHUNK, GROUP_W), BF16)
    for c in range(tm // SGU_CHUNK):
        rows = slice(c * SGU_CHUNK, (c + 1) * SGU_CHUNK)
        stacked = jnp.concatenate([jnp.where(even_head, vn[rows], zero), jnp.where(even_head, zero, vn[rows])], axis=0)
        r = _dot(wsgu_ref[0], stacked)
        sel = jnp.where(lane < 2 * HEAD_W, r[:SGU_CHUNK], r[SGU_CHUNK:])
        sgu_ref[0, rows, :] = (u[rows] * (sel + bsgu_ref[0])).astype(BF16)


def _inproj(h, mods, mod_row, rope, p, layer, *, kv_only, tm):
    b, l, d = h.shape
    names = ["w_cat", "dft_c", "g_q", "w_uq", "g_kv", "w_uk", "w_uv", "v_one", "g_sgu", "ones_bd", "w_sgu", "b_sgu"]
    weights = [p[n] for n in names]
    w_specs = [_const_spec(p[n].shape) if n in SHARED_PARAMS else _layer_spec(p[n].shape[1:], layer) for n in names]
    rope_spec = pl.BlockSpec((tm, HEAD_PAD), lambda i, j: (j, 0))
    tok = lambda w, dt: jax.ShapeDtypeStruct((b, l, w), dt)
    tok_spec = lambda w: pl.BlockSpec((1, tm, w), lambda i, j: (i, j, 0))
    tr = lambda w: jax.ShapeDtypeStruct((b, w, l), BF16)
    tr_spec = lambda w: pl.BlockSpec((1, w, tm), lambda i, j: (i, 0, j))
    if kv_only:
        out_shape = [tok(HEADS * HEAD_PAD, BF16), tr(HEADS * V_PAD)]
        out_specs = [tok_spec(HEADS * HEAD_PAD), tr_spec(HEADS * V_PAD)]
    else:
        out_shape = [tok(2 * GROUP_W, BF16), tr(HEADS * HEAD_PAD), tok(HEADS * HEAD_PAD, BF16),
                     tr(HEADS * V_PAD), tok(GROUP_W, BF16), tok(GROUP_W, F32)]
        out_specs = [tok_spec(2 * GROUP_W), tr_spec(HEADS * HEAD_PAD), tok_spec(HEADS * HEAD_PAD),
                     tr_spec(HEADS * V_PAD), tok_spec(GROUP_W), tok_spec(GROUP_W)]
    kern = functools.partial(_inproj_kernel, kv_only=kv_only, tm=tm,
                             q_scale=float(QK_NOPE + QK_ROPE) ** -0.5 * math.log2(math.e))
    return pl.pallas_call(
        kern,
        out_shape=out_shape,
        grid=(b, l // tm),
        in_specs=[
            pl.BlockSpec((1, tm, d), lambda i, j: (i, j, 0)),
            _mod_spec(mods, mod_row),
            _layer_spec((1, d), layer),
            rope_spec, rope_spec, rope_spec,
        ] + w_specs,
        out_specs=out_specs,
        compiler_params=_cparams("parallel", "parallel"),
        name="inproj",
    )(h, mods, p["g_mix"], *rope, *weights)


def _dft_kernel(c_ref, s_ref, z_ref, o_ref):
    za = z_ref[0, :, :GROUP_W]
    zb = z_ref[0, :, GROUP_W:]
    o_ref[0] = (_dot(c_ref[...], za) - _dot(s_ref[...], zb)).astype(BF16)


def _dft(zab, cmat, smat, *, tm):
    b, l, _ = zab.shape
    return pl.pallas_call(
        _dft_kernel,
        out_shape=jax.ShapeDtypeStruct((b, l, GROUP_W), BF16),
        grid=(l // tm, b),
        in_specs=[
            pl.BlockSpec((tm, l), lambda i, j: (i, 0)),
            pl.BlockSpec((tm, l), lambda i, j: (i, 0)),
            pl.BlockSpec((1, l, 2 * GROUP_W), lambda i, j: (j, 0, 0)),
        ],
        out_specs=pl.BlockSpec((1, tm, GROUP_W), lambda i, j: (j, i, 0)),
        compiler_params=_cparams("parallel", "parallel"),
        name="dft",
    )(cmat, smat, zab)


def _dft_fold_kernel(cq_ref, sq_ref, rev_ref, z_ref, o_ref, fold_ref, t_ref, *, l):
    half = l // 2
    nb = half // FLIP
    rev = rev_ref[...]
    fold_ref[0:8, :] = jnp.zeros((8, 2 * GROUP_W), F32)
    for jb in range(nb):
        blk = z_ref[0, (2 * nb - 1 - jb) * FLIP:(2 * nb - jb) * FLIP, :]
        fold_ref[8 + jb * FLIP:8 + (jb + 1) * FLIP, :] = _dot(rev, blk)
    zlo = z_ref[0, 0:half, :].astype(F32)
    mirrored = fold_ref[7:7 + half, :]
    even = (zlo[:, :GROUP_W] + mirrored[:, :GROUP_W]).astype(BF16)
    odd = (zlo[:, GROUP_W:] - mirrored[:, GROUP_W:]).astype(BF16)
    p = _dot(cq_ref[...], even)
    q = _dot(sq_ref[...], odd)
    k_idx = lax.broadcasted_iota(jnp.int32, (p.shape[0], 1), 0)
    sign = (1 - 2 * (k_idx & 1)).astype(F32) * (1.0 / math.sqrt(l))
    p = p + sign * z_ref[0, half:half + 1, :GROUP_W].astype(F32)
    o_ref[0, 0:half, :] = (p[:half] - q[:half]).astype(BF16)
    t_ref[...] = p + q
    upper = t_ref[1:half + 1, :].astype(BF16)
    for jb in range(nb):
        blk = upper[(nb - 1 - jb) * FLIP:(nb - jb) * FLIP, :]
        o_ref[0, half + jb * FLIP:half + (jb + 1) * FLIP, :] = _dot(rev, blk).astype(BF16)


def _dft_fold(zab, cq, sq, rev):
    b, l, _ = zab.shape
    half = l // 2
    rows = half + 16
    return pl.pallas_call(
        functools.partial(_dft_fold_kernel, l=l),
        out_shape=jax.ShapeDtypeStruct((b, l, GROUP_W), BF16),
        grid=(b,),
        in_specs=[
            _const_spec((rows, half)),
            _const_spec((rows, half)),
            _const_spec((FLIP, FLIP)),
            pl.BlockSpec((1, l, 2 * GROUP_W), lambda i: (i, 0, 0)),
        ],
        out_specs=pl.BlockSpec((1, l, GROUP_W), lambda i: (i, 0, 0)),
        scratch_shapes=[pltpu.VMEM((half + 8, 2 * GROUP_W), F32), pltpu.VMEM((rows, GROUP_W), F32)],
        compiler_params=_cparams("parallel"),
        name="dft_fold",
    )(cq, sq, rev, zab)


def _attn_kernel(qt_ref, *refs, seg_lens):
    n_seg = len(seg_lens)
    kv_refs = refs[:2 * n_seg]
    o_ref = refs[2 * n_seg]
    acc_ref = refs[2 * n_seg + 1]
    tq = qt_ref.shape[2]
    chains = [(qs, hd) for qs in range(0, tq, TQ_SUB) for hd in range(HEADS)]
    items = []
    for s, lk in enumerate(seg_lens):
        chunk = min(KEY_CHUNK, lk)
        for c0 in range(0, lk, chunk):
            items += [(s, c0, chunk, ch) for ch in range(len(chains))]
    m = [None] * len(chains)
    acc = [None] * len(chains)
    scores = {}
    for i in range(len(items) + SCORE_LOOKAHEAD):
        if i < len(items):
            s, c0, chunk, ch = items[i]
            qs, hd = chains[ch]
            k = kv_refs[2 * s][0, c0:c0 + chunk, hd * HEAD_PAD:(hd + 1) * HEAD_PAD]
            scores[i] = _dot(k, qt_ref[0, hd * HEAD_PAD:(hd + 1) * HEAD_PAD, qs:qs + TQ_SUB])
        j = i - SCORE_LOOKAHEAD
        if j < 0:
            continue
        s, c0, chunk, ch = items[j]
        qs, hd = chains[ch]
        sc = scores.pop(j)
        cm = sc.max(axis=0, keepdims=True)
        m_new = cm if m[ch] is None else jnp.maximum(m[ch], cm)
        p = jnp.exp2(sc - m_new).astype(BF16)
        vt = kv_refs[2 * s + 1][0, hd * V_PAD:(hd + 1) * V_PAD, c0:c0 + chunk]
        pv = _dot(vt, p)
        acc[ch] = pv if acc[ch] is None else acc[ch] * jnp.exp2(m[ch] - m_new) + pv
        m[ch] = m_new
    for ch, (qs, hd) in enumerate(chains):
        inv = 1.0 / acc[ch][HEAD_W:HEAD_W + 1]
        acc_ref[hd * HEAD_W:(hd + 1) * HEAD_W, qs:qs + TQ_SUB] = acc[ch][:HEAD_W] * inv
    o_ref[0] = acc_ref[...].T.astype(BF16)


def _attention(qt, segments, *, tq):
    b, _, l = qt.shape
    in_specs = [pl.BlockSpec((1, HEADS * HEAD_PAD, tq), lambda i, j: (i, 0, j))]
    args = [qt]
    for k, vt in segments:
        lk = k.shape[1]
        in_specs.append(pl.BlockSpec((1, lk, HEADS * HEAD_PAD), lambda i, j: (i, 0, 0)))
        in_specs.append(pl.BlockSpec((1, HEADS * V_PAD, lk), lambda i, j: (i, 0, 0)))
        args += [k, vt]
    return pl.pallas_call(
        functools.partial(_attn_kernel, seg_lens=tuple(k.shape[1] for k, _ in segments)),
        out_shape=jax.ShapeDtypeStruct((b, l, GROUP_W), BF16),
        grid=(b, l // tq),
        in_specs=in_specs,
        out_specs=pl.BlockSpec((1, tq, GROUP_W), lambda i, j: (i, j, 0)),
        scratch_shapes=[pltpu.VMEM((GROUP_W, tq), F32)],
        compiler_params=_cparams("parallel", "parallel"),
        name="attention",
    )(*args)


def _pool_kernel(z_ref, wp_ref, sp_ref, o_ref, pad_ref, *, l, rb):
    zeros = jnp.zeros((POOL_PAD, GROUP_W), F32)
    pad_ref[0:POOL_PAD, :] = zeros
    pad_ref[POOL_PAD + l:2 * POOL_PAD + l, :] = zeros
    pad_ref[POOL_PAD:POOL_PAD + l, :] = z_ref[0]
    halo = MAX_HALF_WINDOW
    n = rb + 2 * halo
    lane = lax.broadcasted_iota(jnp.int32, (rb, HEAD_PAD), 1)
    first_group = lane < HEAD_W

    def up(a, kk):
        return pltpu.roll(a, n - kk, axis=0)

    def down(a, kk):
        return pltpu.roll(a, kk, axis=0)

    for r in range(l // rb):
        base = POOL_PAD + r * rb - halo
        edge = r == 0 or r == l // rb - 1
        pooled = []
        for tile in range(GROUP_W // HEAD_PAD):
            lanes = slice(tile * HEAD_PAD, (tile + 1) * HEAD_PAD)
            w_a, w_b = POOL_WINDOWS[2 * tile], POOL_WINDOWS[2 * tile + 1]
            x = pad_ref[base:base + n, lanes]
            run = {1: x}
            width = 1
            while width < min(w_b, halo):
                run[2 * width] = run[width] + up(run[width], width)
                width *= 2

            def window(w):
                if w in run:
                    return down(run[w], w // 2)
                return down(run[w // 2], w // 2) + run[w // 2]

            win = jnp.where(first_group, window(w_a)[halo:halo + rb], window(w_b)[halo:halo + rb])
            z = x[halo:halo + rb]
            if edge:
                t = lax.broadcasted_iota(jnp.int32, (rb, HEAD_PAD), 0) + r * rb
                hw = jnp.where(first_group, w_a // 2, w_b // 2)
                cnt = jnp.minimum(t + hw, l) - jnp.maximum(t - hw, 0)
                pooled.append(win / cnt.astype(F32) - z)
            else:
                pooled.append(win * jnp.where(first_group, 1.0 / w_a, 1.0 / w_b) - z)
        y = _dot(jnp.concatenate(pooled, axis=1).astype(BF16), wp_ref[0]) * sp_ref[0]
        o_ref[0, r * rb:(r + 1) * rb, :] = y.astype(BF16)


def _pool(zp, wp_bd, s_pool, layer):
    b, l, _ = zp.shape
    rb = min(l, 256)
    return pl.pallas_call(
        functools.partial(_pool_kernel, l=l, rb=rb),
        out_shape=jax.ShapeDtypeStruct((b, l, GROUP_W), BF16),
        grid=(b,),
        in_specs=[
            pl.BlockSpec((1, l, GROUP_W), lambda i: (i, 0, 0)),
            _layer_spec((GROUP_W, GROUP_W), layer),
            _layer_spec((1, GROUP_W), layer),
        ],
        out_specs=pl.BlockSpec((1, l, GROUP_W), lambda i: (i, 0, 0)),
        scratch_shapes=[pltpu.VMEM((l + 2 * POOL_PAD, GROUP_W), F32)],
        compiler_params=_cparams("parallel"),
        name="pool",
    )(zp, wp_bd, s_pool)


def _block_diag(blocks):
    depth, n, r, c = blocks.shape
    rows = [jnp.concatenate([blocks[:, i] if j == i else jnp.zeros((depth, r, c), blocks.dtype)
                             for j in range(n)], axis=2) for i in range(n)]
    return jnp.concatenate(rows, axis=1)


def _channel_dft():
    k = np.arange(HEAD_W)
    ang = 2.0 * np.pi * np.outer(k, k) / HEAD_W
    eye = np.eye(HEADS)
    c = np.kron(eye, np.cos(ang)) / math.sqrt(HEAD_W)
    s = np.kron(eye, np.sin(ang)) / math.sqrt(HEAD_W)
    return jnp.asarray(np.concatenate([c, s], axis=1), F32).astype(BF16)


def _token_dft(l, rows=None, cols=None):
    kk = jnp.arange(l if rows is None else rows, dtype=jnp.int32)
    tt = jnp.arange(l if cols is None else cols, dtype=jnp.int32)
    ang = ((kk[:, None] * tt[None, :]) % l).astype(F32) * (2.0 * math.pi / l)
    norm = 1.0 / math.sqrt(l)
    return (jnp.cos(ang) * norm).astype(BF16), (jnp.sin(ang) * norm).astype(BF16)


def _dft_table_kernel(ca_ref, sa_ref, cb_ref, sb_ref, c_ref, s_ref):
    ca, sa = ca_ref[0], sa_ref[0]
    cb, sb = cb_ref[...], sb_ref[...]
    c_ref[...] = (ca * cb - sa * sb).astype(BF16)
    s_ref[...] = (sa * cb + ca * sb).astype(BF16)


def _folded_token_dft(l):
    half = l // 2
    blocks = pl.cdiv(half + 1, TABLE_ROWS)
    tt = jnp.arange(half, dtype=jnp.int32)[None, :]
    angle = lambda kk: ((kk[:, None] * tt) % l).astype(F32) * (2.0 * math.pi / l)
    coarse = angle(jnp.arange(blocks, dtype=jnp.int32) * TABLE_ROWS)
    fine = angle(jnp.arange(TABLE_ROWS, dtype=jnp.int32))
    norm = 1.0 / math.sqrt(l)
    row_spec = pl.BlockSpec((1, 1, half), lambda i: (i, 0, 0))
    out_spec = pl.BlockSpec((TABLE_ROWS, half), lambda i: (i, 0))
    table = jax.ShapeDtypeStruct((blocks * TABLE_ROWS, half), BF16)
    cq, sq = pl.pallas_call(
        _dft_table_kernel,
        out_shape=[table, table],
        grid=(blocks,),
        in_specs=[row_spec, row_spec, _const_spec((TABLE_ROWS, half)), _const_spec((TABLE_ROWS, half))],
        out_specs=[out_spec, out_spec],
        compiler_params=_cparams("parallel"),
        name="dft_tables",
    )(jnp.cos(coarse)[:, None, :], jnp.sin(coarse)[:, None, :], jnp.cos(fine) * norm, jnp.sin(fine) * norm)
    rev = jnp.asarray(np.eye(FLIP)[::-1].copy(), F32).astype(BF16)
    return cq, sq, rev


def _rope_tables(l):
    pos = jnp.arange(l)
    row = (pos // GRID_W).astype(F32)
    col = (pos % GRID_W).astype(F32)
    inv = jnp.power(ROPE_BASE, -jnp.arange(0, AXIS_ROPE, 2, dtype=F32) / AXIS_ROPE)
    zero = jnp.zeros((l, ROPE_PAIR), F32)
    row_sin, col_sin = jnp.sin(row[:, None] * inv), jnp.sin(col[:, None] * inv)
    row_cos, col_cos = jnp.cos(row[:, None] * inv), jnp.cos(col[:, None] * inv)
    place = lambda base, parts: base.at[:, ROPE_LANE0:ROPE_LANE0 + QK_ROPE].set(jnp.concatenate(parts, axis=1))
    cos = place(jnp.ones((l, HEAD_PAD), F32), [row_cos, row_cos, col_cos, col_cos])
    sin_lo = place(jnp.zeros((l, HEAD_PAD), F32), [-row_sin, zero, -col_sin, zero])
    sin_hi = place(jnp.zeros((l, HEAD_PAD), F32), [zero, row_sin, zero, col_sin])
    return cos, sin_lo, sin_hi


def _identity_rope_tables(l):
    zero = jnp.zeros((l, HEAD_PAD), F32)
    return jnp.ones((l, HEAD_PAD), F32), zero, zero


SHARED_PARAMS = ("dft_c", "v_one", "ones_bd")


def _prep_params(g_ffn1, g_mix, w_in, w_fnet, g_q, w_uq, g_kv, w_ukv,
                 g_sgu, w_sgu, b_sgu, w_pool, s_pool, g_ffn2):
    depth = w_in.shape[0]
    zeros = lambda *shape: jnp.zeros((depth,) + shape, F32)
    p = {}
    p["g_1"] = g_ffn1.reshape(depth, 1, D_MODEL)
    p["g_2"] = g_ffn2.reshape(depth, 1, D_MODEL)
    p["g_mix"] = g_mix.reshape(depth, 1, D_MODEL)
    p["w_cat"] = jnp.concatenate([
        w_in[..., OFF_F:OFF_Q], w_in[..., OFF_G:OFF_P], w_in[..., OFF_P:],
        w_in[..., OFF_Q:OFF_KV], zeros(D_MODEL, Q_PAD - Q_RANK), w_in[..., OFF_KV:OFF_KR],
        zeros(D_MODEL, ROPE_LANE0), w_in[..., OFF_KR:OFF_G],
        zeros(D_MODEL, HEAD_PAD - ROPE_LANE0 - QK_ROPE)], axis=2).astype(BF16)
    p["dft_c"] = _channel_dft()
    p["g_q"] = jnp.concatenate([g_q, zeros(Q_PAD - Q_RANK)], axis=1).reshape(depth, 1, Q_PAD)
    wuq = w_uq.reshape(depth, Q_RANK, HEADS, QK_NOPE + QK_ROPE)
    wuq = jnp.concatenate([wuq, zeros(Q_RANK, HEADS, HEAD_PAD - QK_NOPE - QK_ROPE)], axis=-1)
    p["w_uq"] = jnp.concatenate([wuq.reshape(depth, Q_RANK, HEADS * HEAD_PAD),
                                 zeros(Q_PAD - Q_RANK, HEADS * HEAD_PAD)], axis=1).astype(BF16)
    p["g_kv"] = g_kv.reshape(depth, 1, KV_RANK)
    wukv = w_ukv.reshape(depth, KV_RANK, HEADS, QK_NOPE + HEAD_W)
    p["w_uk"] = jnp.concatenate([wukv[..., :QK_NOPE], zeros(KV_RANK, HEADS, HEAD_PAD - QK_NOPE)],
                                axis=-1).reshape(depth, KV_RANK, HEADS * HEAD_PAD).astype(BF16)
    p["w_uv"] = jnp.concatenate([wukv[..., QK_NOPE:], zeros(KV_RANK, HEADS, V_PAD - HEAD_W)],
                                axis=-1).reshape(depth, KV_RANK, HEADS * V_PAD).astype(BF16)
    v_one = np.zeros((1, HEADS, V_PAD), np.float32)
    v_one[:, :, HEAD_W] = 1.0
    p["v_one"] = jnp.asarray(v_one.reshape(1, HEADS * V_PAD))
    p["g_sgu"] = g_sgu.reshape(depth, 1, GROUP_W)
    p["ones_bd"] = jnp.asarray(np.kron(np.eye(HEADS), np.full((HEAD_W, HEAD_W), 1.0 / HEAD_W)), BF16)
    p["w_sgu"] = jnp.concatenate([jnp.concatenate([w_sgu[:, 0], w_sgu[:, 1]], axis=2),
                                  jnp.concatenate([w_sgu[:, 2], w_sgu[:, 3]], axis=2)], axis=1).astype(BF16)
    p["b_sgu"] = jnp.repeat(jnp.swapaxes(b_sgu, 1, 2), HEAD_W, axis=2)
    p["wf_bd"] = _block_diag(w_fnet).astype(BF16)
    p["wp_bd"] = _block_diag(w_pool).astype(BF16)
    p["s_pool"] = s_pool.reshape(depth, 1, GROUP_W)
    return p


def _mix(h, mods, mod_row, rope, dft, kv_extra, p, layer, *, tm, tq):
    zab, q, k, v, sgu, zp = _inproj(h, mods, mod_row, rope, p, layer, kv_only=False, tm=tm)
    if len(dft) == 3:
        fre = _dft_fold(zab, *dft)
    else:
        fre = _dft(zab, dft[0], dft[1], tm=min(512, h.shape[1]))
    att = _attention(q, [(k, v)] + kv_extra, tq=tq)
    pool = _pool(zp, p["wp_bd"], p["s_pool"], layer)
    return (fre, att, sgu, pool), (k, v)


def kernel(x, c, ctx, c_ctx, w_ada, b_ada, g_ffn1, w13_ffn1, w2_ffn1, g_mix, w_in, w_fnet, g_q, w_uq, g_kv, w_ukv, g_sgu, w_sgu, b_sgu, w_pool, s_pool, w_out, g_ffn2, w13_ffn2, w2_ffn2, g_final):
    b, l, d = x.shape
    lc = ctx.shape[1]
    depth = w_ada.shape[0]
    tm = 1024
    tm_ffn = 1024
    tmc = min(lc, 256)

    cond_rows = 16
    cond = jnp.concatenate([c, c_ctx[None], jnp.zeros((cond_rows - b - 1, d), F32)], axis=0)
    mods = _ada(cond, w_ada, b_ada).reshape(depth * cond_rows, N_MOD, d)

    rope = _rope_tables(l)
    rope_c = _identity_rope_tables(lc)
    dft = _folded_token_dft(l)
    dft_c = _token_dft(lc)

    w13_1, w2_1 = w13_ffn1.astype(BF16), w2_ffn1.astype(BF16)
    w13_2, w2_2 = w13_ffn2.astype(BF16), w2_ffn2.astype(BF16)
    w_out_b = w_out.astype(BF16)

    flat = lambda a: a.reshape(1, b * lc, a.shape[-1])
    tmc_ffn = min(b * lc, tm_ffn)

    p = _prep_params(g_ffn1, g_mix, w_in, w_fnet, g_q, w_uq, g_kv, w_ukv,
                     g_sgu, w_sgu, b_sgu, w_pool, s_pool, g_ffn2)

    h, hc = x, ctx
    for i in range(depth):
        last = i == depth - 1
        m = (i * cond_rows, True)
        mc = (i * cond_rows + b, False)

        h = _ffn(h, mods, m, p["g_1"], w13_1, w2_1, i, row0=0, tm=tm_ffn)
        hc = _ffn(flat(hc), mods, mc, p["g_1"], w13_1, w2_1, i, row0=0, tm=tmc_ffn).reshape(b, lc, d)

        if last:
            kc, vc = _inproj(hc, mods, mc, rope_c, p, i, kv_only=True, tm=tmc)
        else:
            mix_c, (kc, vc) = _mix(hc, mods, mc, rope_c, dft_c, [], p, i, tm=tmc, tq=tmc)
        mix, _ = _mix(h, mods, m, rope, dft, [(kc, vc)], p, i, tm=tm, tq=TQ)
        h = _ffn(h, mods, m, p["g_2"], w13_2, w2_2, i, row0=6, tm=tm_ffn, mix=mix + (p["wf_bd"], w_out_b),
                 g_final=g_final.reshape(1, d) if last else None)
        if not last:
            mix_c = tuple(flat(a) for a in mix_c) + (p["wf_bd"], w_out_b)
            hc = _ffn(flat(hc), mods, mc, p["g_2"], w13_2, w2_2, i, row0=6, tm=tmc_ffn,
                      mix=mix_c).reshape(b, lc, d)
    return h
```

```python
import functools
import math

import numpy as np
import jax
import jax.numpy as jnp
from jax import lax
from jax.experimental import pallas as pl
from jax.experimental.pallas import tpu as pltpu

F32 = jnp.float32
BF16 = jnp.bfloat16

D_MODEL = 1024
DEPTH = 2
GRID_W = 64
EPS = 1e-6
N_MOD = 9
GROUP_W = 256
HEADS = 4
HEAD_W = GROUP_W // HEADS
QK_NOPE = 64
QK_ROPE = 32
AXIS_ROPE = QK_ROPE // 2
Q_RANK = 192
KV_RANK = 128
ROPE_BASE = 10000.0
SGU_CHUNK = 128
POOL_WINDOWS = (2, 4, 8, 16)
D_FF = 2816
HEAD_PAD = 128
V_PAD = 96
KEY_CHUNK = 256
SCORE_LOOKAHEAD = 12
FLIP = 256
TABLE_ROWS = 256
ROPE_LANE0 = QK_NOPE

OFF_F = 0
OFF_Q = OFF_F + GROUP_W
OFF_KV = OFF_Q + Q_RANK
OFF_KR = OFF_KV + KV_RANK
OFF_G = OFF_KR + QK_ROPE
OFF_P = OFF_G + 2 * GROUP_W
IN_W = OFF_P + GROUP_W

CAT_F = 0
CAT_G = CAT_F + GROUP_W
CAT_P = CAT_G + 2 * GROUP_W
CAT_Q = CAT_P + GROUP_W
Q_PAD = 256
CAT_KV = CAT_Q + Q_PAD
CAT_KR = CAT_KV + KV_RANK
CAT_W = CAT_KR + HEAD_PAD
ROPE_PAIR = AXIS_ROPE // 2
TQ = 1024
TQ_SUB = 256

MAX_HALF_WINDOW = max(POOL_WINDOWS) // 2
POOL_BLOCK = 256
VMEM_LIMIT = 56 * 1024 * 1024


def _cparams(*sem):
    return pltpu.CompilerParams(dimension_semantics=sem, vmem_limit_bytes=VMEM_LIMIT)


def _const_spec(shape):
    nd = len(shape)
    return pl.BlockSpec(shape, lambda *_: (0,) * nd, pipeline_mode=pl.Buffered(1))


def _layer_spec(shape, layer):
    nd = len(shape)
    return pl.BlockSpec((1,) + tuple(shape), lambda *_: (layer,) + (0,) * nd, pipeline_mode=pl.Buffered(1))


def _mod_spec(mods, mod_row):
    base, per_batch = mod_row
    return pl.BlockSpec((1,) + mods.shape[1:], lambda i, j: (base + i if per_batch else base, 0, 0))


def _dot(a, b):
    return jnp.dot(a, b, preferred_element_type=F32)


def _rms(x):
    return x * lax.rsqrt(jnp.mean(x * x, axis=-1, keepdims=True) + EPS)


def _norm_mod(x, g, shift, scale):
    return (_rms(x) * g) * (1.0 + scale) + shift


def _ada_kernel(c_ref, w_ref, b_ref, o_ref):
    c = c_ref[...]
    s = c * jax.nn.sigmoid(c)
    hi = s.astype(BF16)
    lo = (s - hi.astype(F32)).astype(BF16)
    r = _dot(jnp.concatenate([hi, lo], axis=0), w_ref[0].astype(BF16))
    rows = c.shape[0]
    o_ref[0] = r[:rows] + r[rows:] + b_ref[0]


def _ada(cond, w_ada, b_ada):
    rows, d = cond.shape
    depth, _, width = w_ada.shape
    bn = 2304
    return pl.pallas_call(
        _ada_kernel,
        out_shape=jax.ShapeDtypeStruct((depth, rows, width), F32),
        grid=(depth, width // bn),
        in_specs=[
            pl.BlockSpec((rows, d), lambda i, j: (0, 0)),
            pl.BlockSpec((1, d, bn), lambda i, j: (i, 0, j)),
            pl.BlockSpec((1, 1, bn), lambda i, j: (i, 0, j)),
        ],
        out_specs=pl.BlockSpec((1, rows, bn), lambda i, j: (i, 0, j)),
        compiler_params=_cparams("parallel", "parallel"),
        name="ada",
    )(cond, w_ada, b_ada.reshape(depth, 1, width))


def _pooled_rows(pad_ref, base, t0, seq_len, clip):
    halo = MAX_HALF_WINDOW
    rb = POOL_BLOCK
    n = rb + 2 * halo
    lane = lax.broadcasted_iota(jnp.int32, (rb, HEAD_PAD), 1)
    first_group = lane < HEAD_W

    def up(a, kk):
        return pltpu.roll(a, n - kk, axis=0)

    def down(a, kk):
        return pltpu.roll(a, kk, axis=0)

    pooled = []
    for tile in range(GROUP_W // HEAD_PAD):
        w_a, w_b = POOL_WINDOWS[2 * tile], POOL_WINDOWS[2 * tile + 1]
        x = pad_ref[base:base + n, tile * HEAD_PAD:(tile + 1) * HEAD_PAD]
        run = {1: x}
        width = 1
        while width < min(w_b, halo):
            run[2 * width] = run[width] + up(run[width], width)
            width *= 2

        def window(w):
            if w in run:
                return down(run[w], w // 2)
            return down(run[w // 2], w // 2) + run[w // 2]

        win = jnp.where(first_group, window(w_a)[halo:halo + rb], window(w_b)[halo:halo + rb])
        z = x[halo:halo + rb]
        if clip:
            t = lax.broadcasted_iota(jnp.int32, (rb, HEAD_PAD), 0) + t0
            hw = jnp.where(first_group, w_a // 2, w_b // 2)
            cnt = jnp.minimum(t + hw, seq_len) - jnp.maximum(t - hw, 0)
            pooled.append(win / cnt.astype(F32) - z)
        else:
            pooled.append(win * jnp.where(first_group, 1.0 / w_a, 1.0 / w_b) - z)
    return jnp.concatenate(pooled, axis=1)


def _pool_tile(zp_ref, prev_ref, next_ref, pad_ref, seq_len):
    halo = MAX_HALF_WINDOW
    tm = zp_ref.shape[1]
    zeros = jnp.zeros((halo, GROUP_W), F32)
    blocks = []
    if seq_len >= tm:
        j = pl.program_id(1)
        pad_ref[0:halo, :] = jnp.where(j == 0, zeros, prev_ref[0])
        pad_ref[halo:halo + tm, :] = zp_ref[0]
        pad_ref[halo + tm:2 * halo + tm, :] = jnp.where(j == seq_len // tm - 1, zeros, next_ref[0])
        for r in range(tm // POOL_BLOCK):
            clip = r == 0 or r == tm // POOL_BLOCK - 1
            blocks.append(_pooled_rows(pad_ref, r * POOL_BLOCK, j * tm + r * POOL_BLOCK, seq_len, clip))
    else:
        pad_ref[0:halo, :] = zeros
        pad_ref[halo + POOL_BLOCK:2 * halo + POOL_BLOCK, :] = zeros
        for r in range(tm // POOL_BLOCK):
            pad_ref[halo:halo + POOL_BLOCK, :] = zp_ref[0, r * POOL_BLOCK:(r + 1) * POOL_BLOCK, :]
            blocks.append(_pooled_rows(pad_ref, 0, 0, seq_len, True))
    return jnp.concatenate(blocks, axis=0)


def _ffn_kernel(h_ref, mod_ref, g_ref, w13_ref, w2_ref, *rest, row0, fchunk, mix_seq, final_norm):
    rest = list(rest)
    mix_refs = [rest.pop(0) for _ in range(10)] if mix_seq else None
    gf_ref = rest.pop(0) if final_norm else None
    o_ref, xn_ref, act_ref = rest[:3]
    shift = mod_ref[0, row0:row0 + 1, :]
    scale = mod_ref[0, row0 + 1:row0 + 2, :]
    gate = mod_ref[0, row0 + 2:row0 + 3, :]
    x = h_ref[0]
    if mix_seq:
        f_ref, att_ref, sgu_ref, zp_ref, zprev_ref, znext_ref, wf_ref, wo_ref, wp_ref, sp_ref = mix_refs
        pooled = _pool_tile(zp_ref, zprev_ref, znext_ref, rest[3], mix_seq)
        pool = (_dot(pooled.astype(BF16), wp_ref[0]) * sp_ref[0]).astype(BF16)
        fmix = _dot(f_ref[0], wf_ref[0]).astype(BF16)
        mixed = jnp.concatenate([fmix, att_ref[0], sgu_ref[0], pool], axis=1)
        x = x + mod_ref[0, 5:6, :] * _dot(mixed, wo_ref[0])
    xn_ref[...] = _norm_mod(x, g_ref[0], shift, scale).astype(BF16)
    for j in range(D_FF // fchunk):
        sl = slice(j * fchunk, (j + 1) * fchunk)
        a = _dot(xn_ref[...], w13_ref[0, :, sl])
        b = _dot(xn_ref[...], w13_ref[0, :, D_FF + j * fchunk:D_FF + (j + 1) * fchunk])
        act_ref[:, sl] = (a * jax.nn.sigmoid(a) * b).astype(BF16)
    y = _dot(act_ref[...], w2_ref[0])
    out = x + (0.5 * gate) * y
    if final_norm:
        out = _rms(out) * gf_ref[...]
    o_ref[0] = out


def _ffn(h, mods, mod_row, g, w13, w2, layer, *, row0, tm, mix=None, mix_seq=0, g_final=None):
    b, l, d = h.shape
    kern = functools.partial(_ffn_kernel, row0=row0, fchunk=256,
                             mix_seq=mix_seq if mix is not None else 0, final_norm=g_final is not None)
    extra, extra_specs, scratch = [], [], []
    if mix is not None:
        fre, att, sgu, zp, wf_bd, w_out, wp_bd, s_pool = mix
        assert mix_seq % tm == 0 or (mix_seq == POOL_BLOCK and tm % POOL_BLOCK == 0)
        halo = MAX_HALF_WINDOW
        per_tile, last = tm // halo, l // halo - 1
        group = pl.BlockSpec((1, tm, GROUP_W), lambda i, j: (i, j, 0))
        prev_rows = pl.BlockSpec((1, halo, GROUP_W), lambda i, j: (i, jnp.maximum(j * per_tile - 1, 0), 0))
        next_rows = pl.BlockSpec((1, halo, GROUP_W), lambda i, j: (i, jnp.minimum((j + 1) * per_tile, last), 0))
        extra += [fre, att, sgu, zp, zp, zp, wf_bd, w_out, wp_bd, s_pool]
        extra_specs += [group, group, group, group, prev_rows, next_rows,
                        _layer_spec((GROUP_W, GROUP_W), layer), _layer_spec((4 * GROUP_W, d), layer),
                        _layer_spec((GROUP_W, GROUP_W), layer), _layer_spec((1, GROUP_W), layer)]
        scratch = [pltpu.VMEM((tm + 2 * halo, GROUP_W), F32)]
    if g_final is not None:
        extra.append(g_final)
        extra_specs.append(_const_spec((1, d)))
    return pl.pallas_call(
        kern,
        out_shape=jax.ShapeDtypeStruct(h.shape, F32),
        grid=(b, l // tm),
        in_specs=[
            pl.BlockSpec((1, tm, d), lambda i, j: (i, j, 0)),
            _mod_spec(mods, mod_row),
            _layer_spec((1, d), layer),
            _layer_spec((d, 2 * D_FF), layer),
            _layer_spec((D_FF, d), layer),
        ] + extra_specs,
        out_specs=pl.BlockSpec((1, tm, d), lambda i, j: (i, j, 0)),
        scratch_shapes=[pltpu.VMEM((tm, d), BF16), pltpu.VMEM((tm, D_FF), BF16)] + scratch,
        compiler_params=_cparams("parallel", "parallel"),
        name="ffn",
    )(h, mods, g, w13, w2, *extra)


def _gelu_tanh(x):
    return 0.5 * x * (1.0 + jnp.tanh(0.7978845608028654 * (x + 0.044715 * (x * x * x))))


def _group_mean(x, ones_bd):
    hi = x.astype(BF16)
    lo = (x - hi.astype(F32)).astype(BF16)
    return _dot(hi, ones_bd) + _dot(lo, ones_bd)


def _rope(x, cos, sin_lo, sin_hi):
    n = x.shape[1]
    up = pltpu.roll(x, n - ROPE_PAIR, axis=1)
    down = pltpu.roll(x, ROPE_PAIR, axis=1)
    return x * cos + up * sin_lo + down * sin_hi


def _inproj_kernel(h_ref, mod_ref, g_ref, cos_ref, slo_ref, shi_ref,
                   wcat_ref, dftc_ref, gq_ref, wuq_ref, gkv_ref, wuk_ref, wuv_ref, vone_ref,
                   gsgu_ref, ones_ref, wsgu_ref, bsgu_ref,
                   *out_refs, kv_only, tm, q_scale):
    if kv_only:
        k_ref, vt_ref = out_refs
    else:
        zab_ref, qt_ref, k_ref, vt_ref, sgu_ref, zp_ref = out_refs
    shift = mod_ref[0, 3:4, :]
    scale = mod_ref[0, 4:5, :]
    n = _norm_mod(h_ref[0], g_ref[0], shift, scale).astype(BF16)
    rope = (cos_ref[...], slo_ref[...], shi_ref[...])

    def proj(c0, c1):
        return _dot(n, wcat_ref[0, :, c0:c1])

    zkv = proj(CAT_KV, CAT_W)
    if not kv_only:
        zq = proj(CAT_Q, CAT_KV)

    kvn = (_rms(zkv[:, :KV_RANK]) * gkv_ref[0]).astype(BF16)
    k = _dot(kvn, wuk_ref[0]) + jnp.concatenate([_rope(zkv[:, KV_RANK:], *rope)] * HEADS, axis=1)
    k_ref[0] = k.astype(BF16)
    vt_ref[0] = (_dot(kvn, wuv_ref[0]) + vone_ref[...]).T.astype(BF16)
    if kv_only:
        return

    zg = proj(CAT_G, CAT_P)

    ms = jnp.sum(zq * zq, axis=-1, keepdims=True) * (1.0 / Q_RANK)
    qn = (zq * lax.rsqrt(ms + EPS) * gq_ref[0]).astype(BF16)
    rope4 = [jnp.concatenate([t] * HEADS, axis=1) for t in rope]
    q = _rope(_dot(qn, wuq_ref[0]), *rope4)
    qt_ref[0] = (q * q_scale).T.astype(BF16)

    zf = proj(CAT_F, CAT_G)
    zp_ref[0] = proj(CAT_P, CAT_Q)

    zab_ref[0] = _dot(zf.astype(BF16), dftc_ref[...]).astype(BF16)

    gz = _gelu_tanh(zg)
    u = gz[:, :GROUP_W]
    vv = gz[:, GROUP_W:]
    ms = _group_mean(vv * vv, ones_ref[...])
    vn = (vv * lax.rsqrt(ms + EPS) * gsgu_ref[0]).astype(BF16)
    lane = lax.broadcasted_iota(jnp.int32, (SGU_CHUNK, GROUP_W), 1)
    even_head = (lane // HEAD_W) % 2 == 0
    zero = jnp.zeros((SGU_CHUNK, GROUP_W), BF16)
    for c in range(tm // SGU_CHUNK):
        rows = slice(c * SGU_CHUNK, (c + 1) * SGU_CHUNK)
        stacked = jnp.concatenate([jnp.where(even_head, vn[rows], zero), jnp.where(even_head, zero, vn[rows])], axis=0)
        r = _dot(wsgu_ref[0], stacked)
        sel = jnp.where(lane < 2 * HEAD_W, r[:SGU_CHUNK], r[SGU_CHUNK:])
        sgu_ref[0, rows, :] = (u[rows] * (sel + bsgu_ref[0])).astype(BF16)


def _inproj(h, mods, mod_row, rope, p, layer, *, kv_only, tm):
    b, l, d = h.shape
    names = ["w_cat", "dft_c", "g_q", "w_uq", "g_kv", "w_uk", "w_uv", "v_one", "g_sgu", "ones_bd", "w_sgu", "b_sgu"]
    weights = [p[n] for n in names]
    w_specs = [_const_spec(p[n].shape) if n in SHARED_PARAMS else _layer_spec(p[n].shape[1:], layer) for n in names]
    rope_spec = pl.BlockSpec((tm, HEAD_PAD), lambda i, j: (j, 0))
    tok = lambda w, dt: jax.ShapeDtypeStruct((b, l, w), dt)
    tok_spec = lambda w: pl.BlockSpec((1, tm, w), lambda i, j: (i, j, 0))
    tr = lambda w: jax.ShapeDtypeStruct((b, w, l), BF16)
    tr_spec = lambda w: pl.BlockSpec((1, w, tm), lambda i, j: (i, 0, j))
    if kv_only:
        out_shape = [tok(HEADS * HEAD_PAD, BF16), tr(HEADS * V_PAD)]
        out_specs = [tok_spec(HEADS * HEAD_PAD), tr_spec(HEADS * V_PAD)]
    else:
        out_shape = [tok(2 * GROUP_W, BF16), tr(HEADS * HEAD_PAD), tok(HEADS * HEAD_PAD, BF16),
                     tr(HEADS * V_PAD), tok(GROUP_W, BF16), tok(GROUP_W, F32)]
        out_specs = [tok_spec(2 * GROUP_W), tr_spec(HEADS * HEAD_PAD), tok_spec(HEADS * HEAD_PAD),
                     tr_spec(HEADS * V_PAD), tok_spec(GROUP_W), tok_spec(GROUP_W)]
    kern = functools.partial(_inproj_kernel, kv_only=kv_only, tm=tm,
                             q_scale=float(QK_NOPE + QK_ROPE) ** -0.5 * math.log2(math.e))
    return pl.pallas_call(
        kern,
        out_shape=out_shape,
        grid=(b, l // tm),
        in_specs=[
            pl.BlockSpec((1, tm, d), lambda i, j: (i, j, 0)),
            _mod_spec(mods, mod_row),
            _layer_spec((1, d), layer),
            rope_spec, rope_spec, rope_spec,
        ] + w_specs,
        out_specs=out_specs,
        compiler_params=_cparams("parallel", "parallel"),
        name="inproj",
    )(h, mods, p["g_mix"], *rope, *weights)


def _dft_kernel(c_ref, s_ref, z_ref, o_ref):
    za = z_ref[0, :, :GROUP_W]
    zb = z_ref[0, :, GROUP_W:]
    o_ref[0] = (_dot(c_ref[...], za) - _dot(s_ref[...], zb)).astype(BF16)


def _dft(zab, cmat, smat, *, tm):
    b, l, _ = zab.shape
    return pl.pallas_call(
        _dft_kernel,
        out_shape=jax.ShapeDtypeStruct((b, l, GROUP_W), BF16),
        grid=(l // tm, b),
        in_specs=[
            pl.BlockSpec((tm, l), lambda i, j: (i, 0)),
            pl.BlockSpec((tm, l), lambda i, j: (i, 0)),
            pl.BlockSpec((1, l, 2 * GROUP_W), lambda i, j: (j, 0, 0)),
        ],
        out_specs=pl.BlockSpec((1, tm, GROUP_W), lambda i, j: (j, i, 0)),
        compiler_params=_cparams("parallel", "parallel"),
        name="dft",
    )(cmat, smat, zab)


def _dft_fold_kernel(cq_ref, sq_ref, rev_ref, z_ref, o_ref, fold_ref, t_ref, *, l):
    half = l // 2
    nb = half // FLIP
    rev = rev_ref[...]
    fold_ref[0:8, :] = jnp.zeros((8, 2 * GROUP_W), F32)
    for jb in range(nb):
        blk = z_ref[0, (2 * nb - 1 - jb) * FLIP:(2 * nb - jb) * FLIP, :]
        fold_ref[8 + jb * FLIP:8 + (jb + 1) * FLIP, :] = _dot(rev, blk)
    zlo = z_ref[0, 0:half, :].astype(F32)
    mirrored = fold_ref[7:7 + half, :]
    even = (zlo[:, :GROUP_W] + mirrored[:, :GROUP_W]).astype(BF16)
    odd = (zlo[:, GROUP_W:] - mirrored[:, GROUP_W:]).astype(BF16)
    p = _dot(cq_ref[...], even)
    q = _dot(sq_ref[...], odd)
    k_idx = lax.broadcasted_iota(jnp.int32, (p.shape[0], 1), 0)
    sign = (1 - 2 * (k_idx & 1)).astype(F32) * (1.0 / math.sqrt(l))
    p = p + sign * z_ref[0, half:half + 1, :GROUP_W].astype(F32)
    o_ref[0, 0:half, :] = (p[:half] - q[:half]).astype(BF16)
    t_ref[...] = p + q
    upper = t_ref[1:half + 1, :].astype(BF16)
    for jb in range(nb):
        blk = upper[(nb - 1 - jb) * FLIP:(nb - jb) * FLIP, :]
        o_ref[0, half + jb * FLIP:half + (jb + 1) * FLIP, :] = _dot(rev, blk).astype(BF16)


def _dft_fold(zab, cq, sq, rev):
    b, l, _ = zab.shape
    half = l // 2
    rows = half + 16
    return pl.pallas_call(
        functools.partial(_dft_fold_kernel, l=l),
        out_shape=jax.ShapeDtypeStruct((b, l, GROUP_W), BF16),
        grid=(b,),
        in_specs=[
            _const_spec((rows, half)),
            _const_spec((rows, half)),
            _const_spec((FLIP, FLIP)),
            pl.BlockSpec((1, l, 2 * GROUP_W), lambda i: (i, 0, 0)),
        ],
        out_specs=pl.BlockSpec((1, l, GROUP_W), lambda i: (i, 0, 0)),
        scratch_shapes=[pltpu.VMEM((half + 8, 2 * GROUP_W), F32), pltpu.VMEM((rows, GROUP_W), F32)],
        compiler_params=_cparams("parallel"),
        name="dft_fold",
    )(cq, sq, rev, zab)


def _attn_kernel(qt_ref, *refs, seg_lens):
    n_seg = len(seg_lens)
    kv_refs = refs[:2 * n_seg]
    o_ref = refs[2 * n_seg]
    acc_ref = refs[2 * n_seg + 1]
    tq = qt_ref.shape[2]
    chains = [(qs, hd) for qs in range(0, tq, TQ_SUB) for hd in range(HEADS)]
    items = []
    for s, lk in enumerate(seg_lens):
        chunk = min(KEY_CHUNK, lk)
        for c0 in range(0, lk, chunk):
            items += [(s, c0, chunk, ch) for ch in range(len(chains))]
    m = [None] * len(chains)
    acc = [None] * len(chains)
    scores = {}
    for i in range(len(items) + SCORE_LOOKAHEAD):
        if i < len(items):
            s, c0, chunk, ch = items[i]
            qs, hd = chains[ch]
            k = kv_refs[2 * s][0, c0:c0 + chunk, hd * HEAD_PAD:(hd + 1) * HEAD_PAD]
            scores[i] = _dot(k, qt_ref[0, hd * HEAD_PAD:(hd + 1) * HEAD_PAD, qs:qs + TQ_SUB])
        j = i - SCORE_LOOKAHEAD
        if j < 0:
            continue
        s, c0, chunk, ch = items[j]
        qs, hd = chains[ch]
        sc = scores.pop(j)
        cm = sc.max(axis=0, keepdims=True)
        m_new = cm if m[ch] is None else jnp.maximum(m[ch], cm)
        p = jnp.exp2(sc - m_new).astype(BF16)
        vt = kv_refs[2 * s + 1][0, hd * V_PAD:(hd + 1) * V_PAD, c0:c0 + chunk]
        pv = _dot(vt, p)
        acc[ch] = pv if acc[ch] is None else acc[ch] * jnp.exp2(m[ch] - m_new) + pv
        m[ch] = m_new
    for ch, (qs, hd) in enumerate(chains):
        inv = 1.0 / acc[ch][HEAD_W:HEAD_W + 1]
        acc_ref[hd * HEAD_W:(hd + 1) * HEAD_W, qs:qs + TQ_SUB] = acc[ch][:HEAD_W] * inv
    o_ref[0] = acc_ref[...].T.astype(BF16)


def _attention(qt, segments, *, tq):
    b, _, l = qt.shape
    in_specs = [pl.BlockSpec((1, HEADS * HEAD_PAD, tq), lambda i, j: (i, 0, j))]
    args = [qt]
    for k, vt in segments:
        lk = k.shape[1]
        in_specs.append(pl.BlockSpec((1, lk, HEADS * HEAD_PAD), lambda i, j: (i, 0, 0)))
        in_specs.append(pl.BlockSpec((1, HEADS * V_PAD, lk), lambda i, j: (i, 0, 0)))
        args += [k, vt]
    return pl.pallas_call(
        functools.partial(_attn_kernel, seg_lens=tuple(k.shape[1] for k, _ in segments)),
        out_shape=jax.ShapeDtypeStruct((b, l, GROUP_W), BF16),
        grid=(b, l // tq),
        in_specs=in_specs,
        out_specs=pl.BlockSpec((1, tq, GROUP_W), lambda i, j: (i, j, 0)),
        scratch_shapes=[pltpu.VMEM((GROUP_W, tq), F32)],
        compiler_params=_cparams("parallel", "parallel"),
        name="attention",
    )(*args)


def _block_diag(blocks):
    depth, n, r, c = blocks.shape
    rows = [jnp.concatenate([blocks[:, i] if j == i else jnp.zeros((depth, r, c), blocks.dtype)
                             for j in range(n)], axis=2) for i in range(n)]
    return jnp.concatenate(rows, axis=1)


def _channel_dft():
    k = np.arange(HEAD_W)
    ang = 2.0 * np.pi * np.outer(k, k) / HEAD_W
    eye = np.eye(HEADS)
    c = np.kron(eye, np.cos(ang)) / math.sqrt(HEAD_W)
    s = np.kron(eye, np.sin(ang)) / math.sqrt(HEAD_W)
    return jnp.asarray(np.concatenate([c, s], axis=1), F32).astype(BF16)


def _token_dft(l, rows=None, cols=None):
    kk = jnp.arange(l if rows is None else rows, dtype=jnp.int32)
    tt = jnp.arange(l if cols is None else cols, dtype=jnp.int32)
    ang = ((kk[:, None] * tt[None, :]) % l).astype(F32) * (2.0 * math.pi / l)
    norm = 1.0 / math.sqrt(l)
    return (jnp.cos(ang) * norm).astype(BF16), (jnp.sin(ang) * norm).astype(BF16)


def _dft_table_kernel(ca_ref, sa_ref, cb_ref, sb_ref, c_ref, s_ref):
    ca, sa = ca_ref[0], sa_ref[0]
    cb, sb = cb_ref[...], sb_ref[...]
    c_ref[...] = (ca * cb - sa * sb).astype(BF16)
    s_ref[...] = (sa * cb + ca * sb).astype(BF16)


def _folded_token_dft(l):
    half = l // 2
    blocks = pl.cdiv(half + 1, TABLE_ROWS)
    tt = jnp.arange(half, dtype=jnp.int32)[None, :]
    angle = lambda kk: ((kk[:, None] * tt) % l).astype(F32) * (2.0 * math.pi / l)
    coarse = angle(jnp.arange(blocks, dtype=jnp.int32) * TABLE_ROWS)
    fine = angle(jnp.arange(TABLE_ROWS, dtype=jnp.int32))
    norm = 1.0 / math.sqrt(l)
    row_spec = pl.BlockSpec((1, 1, half), lambda i: (i, 0, 0))
    out_spec = pl.BlockSpec((TABLE_ROWS, half), lambda i: (i, 0))
    table = jax.ShapeDtypeStruct((blocks * TABLE_ROWS, half), BF16)
    cq, sq = pl.pallas_call(
        _dft_table_kernel,
        out_shape=[table, table],
        grid=(blocks,),
        in_specs=[row_spec, row_spec, _const_spec((TABLE_ROWS, half)), _const_spec((TABLE_ROWS, half))],
        out_specs=[out_spec, out_spec],
        compiler_params=_cparams("parallel"),
        name="dft_tables",
    )(jnp.cos(coarse)[:, None, :], jnp.sin(coarse)[:, None, :], jnp.cos(fine) * norm, jnp.sin(fine) * norm)
    rev = jnp.asarray(np.eye(FLIP)[::-1].copy(), F32).astype(BF16)
    return cq, sq, rev


def _rope_tables(l):
    pos = jnp.arange(l)
    row = (pos // GRID_W).astype(F32)
    col = (pos % GRID_W).astype(F32)
    inv = jnp.power(ROPE_BASE, -jnp.arange(0, AXIS_ROPE, 2, dtype=F32) / AXIS_ROPE)
    zero = jnp.zeros((l, ROPE_PAIR), F32)
    row_sin, col_sin = jnp.sin(row[:, None] * inv), jnp.sin(col[:, None] * inv)
    row_cos, col_cos = jnp.cos(row[:, None] * inv), jnp.cos(col[:, None] * inv)
    place = lambda base, parts: base.at[:, ROPE_LANE0:ROPE_LANE0 + QK_ROPE].set(jnp.concatenate(parts, axis=1))
    cos = place(jnp.ones((l, HEAD_PAD), F32), [row_cos, row_cos, col_cos, col_cos])
    sin_lo = place(jnp.zeros((l, HEAD_PAD), F32), [-row_sin, zero, -col_sin, zero])
    sin_hi = place(jnp.zeros((l, HEAD_PAD), F32), [zero, row_sin, zero, col_sin])
    return cos, sin_lo, sin_hi


def _identity_rope_tables(l):
    zero = jnp.zeros((l, HEAD_PAD), F32)
    return jnp.ones((l, HEAD_PAD), F32), zero, zero


SHARED_PARAMS = ("dft_c", "v_one", "ones_bd")


def _prep_params(g_ffn1, g_mix, w_in, w_fnet, g_q, w_uq, g_kv, w_ukv,
                 g_sgu, w_sgu, b_sgu, w_pool, s_pool, g_ffn2):
    depth = w_in.shape[0]
    zeros = lambda *shape: jnp.zeros((depth,) + shape, F32)
    p = {}
    p["g_1"] = g_ffn1.reshape(depth, 1, D_MODEL)
    p["g_2"] = g_ffn2.reshape(depth, 1, D_MODEL)
    p["g_mix"] = g_mix.reshape(depth, 1, D_MODEL)
    p["w_cat"] = jnp.concatenate([
        w_in[..., OFF_F:OFF_Q], w_in[..., OFF_G:OFF_P], w_in[..., OFF_P:],
        w_in[..., OFF_Q:OFF_KV], zeros(D_MODEL, Q_PAD - Q_RANK), w_in[..., OFF_KV:OFF_KR],
        zeros(D_MODEL, ROPE_LANE0), w_in[..., OFF_KR:OFF_G],
        zeros(D_MODEL, HEAD_PAD - ROPE_LANE0 - QK_ROPE)], axis=2).astype(BF16)
    p["dft_c"] = _channel_dft()
    p["g_q"] = jnp.concatenate([g_q, zeros(Q_PAD - Q_RANK)], axis=1).reshape(depth, 1, Q_PAD)
    wuq = w_uq.reshape(depth, Q_RANK, HEADS, QK_NOPE + QK_ROPE)
    wuq = jnp.concatenate([wuq, zeros(Q_RANK, HEADS, HEAD_PAD - QK_NOPE - QK_ROPE)], axis=-1)
    p["w_uq"] = jnp.concatenate([wuq.reshape(depth, Q_RANK, HEADS * HEAD_PAD),
                                 zeros(Q_PAD - Q_RANK, HEADS * HEAD_PAD)], axis=1).astype(BF16)
    p["g_kv"] = g_kv.reshape(depth, 1, KV_RANK)
    wukv = w_ukv.reshape(depth, KV_RANK, HEADS, QK_NOPE + HEAD_W)
    p["w_uk"] = jnp.concatenate([wukv[..., :QK_NOPE], zeros(KV_RANK, HEADS, HEAD_PAD - QK_NOPE)],
                                axis=-1).reshape(depth, KV_RANK, HEADS * HEAD_PAD).astype(BF16)
    p["w_uv"] = jnp.concatenate([wukv[..., QK_NOPE:], zeros(KV_RANK, HEADS, V_PAD - HEAD_W)],
                                axis=-1).reshape(depth, KV_RANK, HEADS * V_PAD).astype(BF16)
    v_one = np.zeros((1, HEADS, V_PAD), np.float32)
    v_one[:, :, HEAD_W] = 1.0
    p["v_one"] = jnp.asarray(v_one.reshape(1, HEADS * V_PAD))
    p["g_sgu"] = g_sgu.reshape(depth, 1, GROUP_W)
    p["ones_bd"] = jnp.asarray(np.kron(np.eye(HEADS), np.full((HEAD_W, HEAD_W), 1.0 / HEAD_W)), BF16)
    p["w_sgu"] = jnp.concatenate([jnp.concatenate([w_sgu[:, 0], w_sgu[:, 1]], axis=2),
                                  jnp.concatenate([w_sgu[:, 2], w_sgu[:, 3]], axis=2)], axis=1).astype(BF16)
    p["b_sgu"] = jnp.repeat(jnp.swapaxes(b_sgu, 1, 2), HEAD_W, axis=2)
    p["wf_bd"] = _block_diag(w_fnet).astype(BF16)
    p["wp_bd"] = _block_diag(w_pool).astype(BF16)
    p["s_pool"] = s_pool.reshape(depth, 1, GROUP_W)
    return p


def _mix(h, mods, mod_row, rope, dft, kv_extra, p, layer, *, tm, tq):
    zab, q, k, v, sgu, zp = _inproj(h, mods, mod_row, rope, p, layer, kv_only=False, tm=tm)
    if len(dft) == 3:
        fre = _dft_fold(zab, *dft)
    else:
        fre = _dft(zab, dft[0], dft[1], tm=min(512, h.shape[1]))
    att = _attention(q, [(k, v)] + kv_extra, tq=tq)
    return (fre, att, sgu, zp), (k, v)


def kernel(x, c, ctx, c_ctx, w_ada, b_ada, g_ffn1, w13_ffn1, w2_ffn1, g_mix, w_in, w_fnet, g_q, w_uq, g_kv, w_ukv, g_sgu, w_sgu, b_sgu, w_pool, s_pool, w_out, g_ffn2, w13_ffn2, w2_ffn2, g_final):
    b, l, d = x.shape
    lc = ctx.shape[1]
    depth = w_ada.shape[0]
    tm = 1024
    tm_ffn = 1024
    tmc = min(lc, 256)

    cond_rows = 16
    cond = jnp.concatenate([c, c_ctx[None], jnp.zeros((cond_rows - b - 1, d), F32)], axis=0)
    mods = _ada(cond, w_ada, b_ada).reshape(depth * cond_rows, N_MOD, d)

    rope = _rope_tables(l)
    rope_c = _identity_rope_tables(lc)
    dft = _folded_token_dft(l)
    dft_c = _token_dft(lc)

    w13_1, w2_1 = w13_ffn1.astype(BF16), w2_ffn1.astype(BF16)
    w13_2, w2_2 = w13_ffn2.astype(BF16), w2_ffn2.astype(BF16)
    w_out_b = w_out.astype(BF16)

    flat = lambda a: a.reshape(1, b * lc, a.shape[-1])
    tmc_ffn = min(b * lc, tm_ffn)

    p = _prep_params(g_ffn1, g_mix, w_in, w_fnet, g_q, w_uq, g_kv, w_ukv,
                     g_sgu, w_sgu, b_sgu, w_pool, s_pool, g_ffn2)

    h, hc = x, ctx
    for i in range(depth):
        last = i == depth - 1
        m = (i * cond_rows, True)
        mc = (i * cond_rows + b, False)

        h = _ffn(h, mods, m, p["g_1"], w13_1, w2_1, i, row0=0, tm=tm_ffn)
        hc = _ffn(flat(hc), mods, mc, p["g_1"], w13_1, w2_1, i, row0=0, tm=tmc_ffn).reshape(b, lc, d)

        if last:
            kc, vc = _inproj(hc, mods, mc, rope_c, p, i, kv_only=True, tm=tmc)
        else:
            mix_c, (kc, vc) = _mix(hc, mods, mc, rope_c, dft_c, [], p, i, tm=tmc, tq=tmc)
        mix, _ = _mix(h, mods, m, rope, dft, [(kc, vc)], p, i, tm=tm, tq=TQ)
        mix_w = (p["wf_bd"], w_out_b, p["wp_bd"], p["s_pool"])
        h = _ffn(h, mods, m, p["g_2"], w13_2, w2_2, i, row0=6, tm=tm_ffn, mix=mix + mix_w, mix_seq=l,
                 g_final=g_final.reshape(1, d) if last else None)
        if not last:
            mix_c = tuple(flat(a) for a in mix_c) + mix_w
            hc = _ffn(flat(hc), mods, mc, p["g_2"], w13_2, w2_2, i, row0=6, tm=tmc_ffn,
                      mix=mix_c, mix_seq=lc).reshape(b, lc, d)
    return h
```

```python
import functools
import math

import numpy as np
import jax
import jax.numpy as jnp
from jax import lax
from jax.experimental import pallas as pl
from jax.experimental.pallas import tpu as pltpu

F32 = jnp.float32
BF16 = jnp.bfloat16

D_MODEL = 1024
GRID_W = 64
EPS = 1e-6
N_MOD = 9
GROUP_W = 256
HEADS = 4
HEAD_W = GROUP_W // HEADS
QK_NOPE = 64
QK_ROPE = 32
AXIS_ROPE = QK_ROPE // 2
Q_RANK = 192
KV_RANK = 128
ROPE_BASE = 10000.0
SGU_CHUNK = 128
POOL_WINDOWS = (2, 4, 8, 16)
D_FF = 2816
HEAD_PAD = 128
V_PAD = 96
KEY_CHUNK = 256
SCORE_LOOKAHEAD = 12
FLIP = 256
TABLE_ROWS = 128
ROPE_LANE0 = QK_NOPE

OFF_F = 0
OFF_Q = OFF_F + GROUP_W
OFF_KV = OFF_Q + Q_RANK
OFF_KR = OFF_KV + KV_RANK
OFF_G = OFF_KR + QK_ROPE
OFF_P = OFF_G + 2 * GROUP_W

CAT_F = 0
CAT_G = CAT_F + GROUP_W
CAT_P = CAT_G + 2 * GROUP_W
CAT_Q = CAT_P + GROUP_W
Q_PAD = 256
CAT_KV = CAT_Q + Q_PAD
CAT_KR = CAT_KV + KV_RANK
CAT_W = CAT_KR + HEAD_PAD
ROPE_PAIR = AXIS_ROPE // 2
TQ = 1024
TQ_SUB = 256

MAX_HALF_WINDOW = max(POOL_WINDOWS) // 2
POOL_BLOCK = 256
VMEM_LIMIT = 56 * 1024 * 1024


def _cparams(*sem):
    return pltpu.CompilerParams(dimension_semantics=sem, vmem_limit_bytes=VMEM_LIMIT)


def _const_spec(shape):
    nd = len(shape)
    return pl.BlockSpec(shape, lambda *_: (0,) * nd, pipeline_mode=pl.Buffered(1))


def _layer_spec(shape, layer):
    nd = len(shape)
    return pl.BlockSpec((1,) + tuple(shape), lambda *_: (layer,) + (0,) * nd, pipeline_mode=pl.Buffered(1))


def _mod_spec(mods, mod_row):
    base, per_batch = mod_row
    return pl.BlockSpec((1,) + mods.shape[1:], lambda i, j: (base + i if per_batch else base, 0, 0))


def _dot(a, b):
    return jnp.dot(a, b, preferred_element_type=F32)


def _rms(x):
    return x * lax.rsqrt(jnp.mean(x * x, axis=-1, keepdims=True) + EPS)


def _norm_mod(x, g, shift, scale):
    return (_rms(x) * g) * (1.0 + scale) + shift


def _ada_kernel(c_ref, w_ref, b_ref, o_ref):
    c = c_ref[...]
    s = c * jax.nn.sigmoid(c)
    hi = s.astype(BF16)
    lo = (s - hi.astype(F32)).astype(BF16)
    r = _dot(jnp.concatenate([hi, lo], axis=0), w_ref[0].astype(BF16))
    rows = c.shape[0]
    o_ref[0] = r[:rows] + r[rows:] + b_ref[0]


def _ada(cond, w_ada, b_ada):
    rows, d = cond.shape
    depth, _, width = w_ada.shape
    bn = 2304
    return pl.pallas_call(
        _ada_kernel,
        out_shape=jax.ShapeDtypeStruct((depth, rows, width), F32),
        grid=(depth, width // bn),
        in_specs=[
            pl.BlockSpec((rows, d), lambda i, j: (0, 0)),
            pl.BlockSpec((1, d, bn), lambda i, j: (i, 0, j)),
            pl.BlockSpec((1, 1, bn), lambda i, j: (i, 0, j)),
        ],
        out_specs=pl.BlockSpec((1, rows, bn), lambda i, j: (i, 0, j)),
        compiler_params=_cparams("parallel", "parallel"),
        name="ada",
    )(cond, w_ada, b_ada.reshape(depth, 1, width))


def _pooled_rows(pad_ref, base, t0, seq_len, clip):
    halo = MAX_HALF_WINDOW
    rb = POOL_BLOCK
    n = rb + 2 * halo
    lane = lax.broadcasted_iota(jnp.int32, (rb, HEAD_PAD), 1)
    first_group = lane < HEAD_W

    def up(a, kk):
        return pltpu.roll(a, n - kk, axis=0)

    def down(a, kk):
        return pltpu.roll(a, kk, axis=0)

    pooled = []
    for tile in range(GROUP_W // HEAD_PAD):
        w_a, w_b = POOL_WINDOWS[2 * tile], POOL_WINDOWS[2 * tile + 1]
        x = pad_ref[base:base + n, tile * HEAD_PAD:(tile + 1) * HEAD_PAD]
        run = {1: x}
        width = 1
        while width < min(w_b, halo):
            run[2 * width] = run[width] + up(run[width], width)
            width *= 2

        def window(w):
            if w in run:
                return down(run[w], w // 2)
            return down(run[w // 2], w // 2) + run[w // 2]

        win = jnp.where(first_group, window(w_a)[halo:halo + rb], window(w_b)[halo:halo + rb])
        z = x[halo:halo + rb]
        if clip:
            t = lax.broadcasted_iota(jnp.int32, (rb, HEAD_PAD), 0) + t0
            hw = jnp.where(first_group, w_a // 2, w_b // 2)
            cnt = jnp.minimum(t + hw, seq_len) - jnp.maximum(t - hw, 0)
            pooled.append(win / cnt.astype(F32) - z)
        else:
            pooled.append(win * jnp.where(first_group, 1.0 / w_a, 1.0 / w_b) - z)
    return jnp.concatenate(pooled, axis=1)


def _pool_tile(zp_ref, prev_ref, next_ref, pad_ref, seq_len):
    halo = MAX_HALF_WINDOW
    tm = zp_ref.shape[1]
    zeros = jnp.zeros((halo, GROUP_W), F32)
    blocks = []
    if seq_len >= tm:
        j = pl.program_id(1)
        pad_ref[0:halo, :] = jnp.where(j == 0, zeros, prev_ref[0])
        pad_ref[halo:halo + tm, :] = zp_ref[0]
        pad_ref[halo + tm:2 * halo + tm, :] = jnp.where(j == seq_len // tm - 1, zeros, next_ref[0])
        for r in range(tm // POOL_BLOCK):
            clip = r == 0 or r == tm // POOL_BLOCK - 1
            blocks.append(_pooled_rows(pad_ref, r * POOL_BLOCK, j * tm + r * POOL_BLOCK, seq_len, clip))
    else:
        pad_ref[0:halo, :] = zeros
        pad_ref[halo + POOL_BLOCK:2 * halo + POOL_BLOCK, :] = zeros
        for r in range(tm // POOL_BLOCK):
            pad_ref[halo:halo + POOL_BLOCK, :] = zp_ref[0, r * POOL_BLOCK:(r + 1) * POOL_BLOCK, :]
            blocks.append(_pooled_rows(pad_ref, 0, 0, seq_len, True))
    return jnp.concatenate(blocks, axis=0)


def _ffn_kernel(h_ref, mod_ref, g_ref, w13_ref, w2_ref, *rest, row0, fchunk, mix_seq, final_norm):
    rest = list(rest)
    mix_refs = [rest.pop(0) for _ in range(10)] if mix_seq else None
    gf_ref = rest.pop(0) if final_norm else None
    o_ref, xn_ref, act_ref = rest[:3]
    shift = mod_ref[0, row0:row0 + 1, :]
    scale = mod_ref[0, row0 + 1:row0 + 2, :]
    gate = mod_ref[0, row0 + 2:row0 + 3, :]
    x = h_ref[0]
    if mix_seq:
        f_ref, att_ref, sgu_ref, zp_ref, zprev_ref, znext_ref, wf_ref, wo_ref, wp_ref, sp_ref = mix_refs
        pooled = _pool_tile(zp_ref, zprev_ref, znext_ref, rest[3], mix_seq)
        pool = (_dot(pooled.astype(BF16), wp_ref[0]) * sp_ref[0]).astype(BF16)
        fmix = _dot(f_ref[0], wf_ref[0]).astype(BF16)
        mixed = jnp.concatenate([fmix, att_ref[0], sgu_ref[0], pool], axis=1)
        x = x + mod_ref[0, 5:6, :] * _dot(mixed, wo_ref[0])
    xn_ref[...] = _norm_mod(x, g_ref[0], shift, scale).astype(BF16)
    for j in range(D_FF // fchunk):
        sl = slice(j * fchunk, (j + 1) * fchunk)
        a = _dot(xn_ref[...], w13_ref[0, :, sl])
        b = _dot(xn_ref[...], w13_ref[0, :, D_FF + j * fchunk:D_FF + (j + 1) * fchunk])
        act_ref[:, sl] = (a * jax.nn.sigmoid(a) * b).astype(BF16)
    y = _dot(act_ref[...], w2_ref[0])
    out = x + (0.5 * gate) * y
    if final_norm:
        out = _rms(out) * gf_ref[...]
    o_ref[0] = out


def _ffn(h, mods, mod_row, g, w13, w2, layer, *, row0, tm, mix=None, mix_seq=0, g_final=None):
    b, l, d = h.shape
    kern = functools.partial(_ffn_kernel, row0=row0, fchunk=256,
                             mix_seq=mix_seq if mix is not None else 0, final_norm=g_final is not None)
    extra, extra_specs, scratch = [], [], []
    if mix is not None:
        fre, att, sgu, zp, wf_bd, w_out, wp_bd, s_pool = mix
        assert mix_seq % tm == 0 or (mix_seq == POOL_BLOCK and tm % POOL_BLOCK == 0)
        halo = MAX_HALF_WINDOW
        per_tile, last = tm // halo, l // halo - 1
        group = pl.BlockSpec((1, tm, GROUP_W), lambda i, j: (i, j, 0))
        prev_rows = pl.BlockSpec((1, halo, GROUP_W), lambda i, j: (i, jnp.maximum(j * per_tile - 1, 0), 0))
        next_rows = pl.BlockSpec((1, halo, GROUP_W), lambda i, j: (i, jnp.minimum((j + 1) * per_tile, last), 0))
        extra += [fre, att, sgu, zp, zp, zp, wf_bd, w_out, wp_bd, s_pool]
        extra_specs += [group, group, group, group, prev_rows, next_rows,
                        _layer_spec((GROUP_W, GROUP_W), layer), _layer_spec((4 * GROUP_W, d), layer),
                        _layer_spec((GROUP_W, GROUP_W), layer), _layer_spec((1, GROUP_W), layer)]
        scratch = [pltpu.VMEM((tm + 2 * halo, GROUP_W), F32)]
    if g_final is not None:
        extra.append(g_final)
        extra_specs.append(_const_spec((1, d)))
    return pl.pallas_call(
        kern,
        out_shape=jax.ShapeDtypeStruct(h.shape, F32),
        grid=(b, l // tm),
        in_specs=[
            pl.BlockSpec((1, tm, d), lambda i, j: (i, j, 0)),
            _mod_spec(mods, mod_row),
            _layer_spec((1, d), layer),
            _layer_spec((d, 2 * D_FF), layer),
            _layer_spec((D_FF, d), layer),
        ] + extra_specs,
        out_specs=pl.BlockSpec((1, tm, d), lambda i, j: (i, j, 0)),
        scratch_shapes=[pltpu.VMEM((tm, d), BF16), pltpu.VMEM((tm, D_FF), BF16)] + scratch,
        compiler_params=_cparams("parallel", "parallel"),
        name="ffn",
    )(h, mods, g, w13, w2, *extra)


def _gelu_tanh(x):
    return 0.5 * x * (1.0 + jnp.tanh(0.7978845608028654 * (x + 0.044715 * (x * x * x))))


def _group_mean(x, ones_bd):
    hi = x.astype(BF16)
    lo = (x - hi.astype(F32)).astype(BF16)
    return _dot(hi, ones_bd) + _dot(lo, ones_bd)


def _rope(x, cos, sin_lo, sin_hi):
    n = x.shape[1]
    up = pltpu.roll(x, n - ROPE_PAIR, axis=1)
    down = pltpu.roll(x, ROPE_PAIR, axis=1)
    return x * cos + up * sin_lo + down * sin_hi


def _inproj_kernel(h_ref, mod_ref, g_ref, cos_ref, slo_ref, shi_ref,
                   wcat_ref, dftc_ref, gq_ref, wuq_ref, gkv_ref, wuk_ref, wuv_ref, vone_ref,
                   gsgu_ref, ones_ref, wsgu_ref, bsgu_ref,
                   *out_refs, kv_only, tm, q_scale):
    if kv_only:
        k_ref, vt_ref = out_refs
    else:
        zab_ref, qt_ref, k_ref, vt_ref, sgu_ref, zp_ref = out_refs
    shift = mod_ref[0, 3:4, :]
    scale = mod_ref[0, 4:5, :]
    n = _norm_mod(h_ref[0], g_ref[0], shift, scale).astype(BF16)
    rope = (cos_ref[...], slo_ref[...], shi_ref[...])

    def proj(c0, c1):
        return _dot(n, wcat_ref[0, :, c0:c1])

    zkv = proj(CAT_KV, CAT_W)
    if not kv_only:
        zq = proj(CAT_Q, CAT_KV)

    kvn = (_rms(zkv[:, :KV_RANK]) * gkv_ref[0]).astype(BF16)
    k = _dot(kvn, wuk_ref[0]) + jnp.concatenate([_rope(zkv[:, KV_RANK:], *rope)] * HEADS, axis=1)
    k_ref[0] = k.astype(BF16)
    vt_ref[0] = (_dot(kvn, wuv_ref[0]) + vone_ref[...]).T.astype(BF16)
    if kv_only:
        return

    zg = proj(CAT_G, CAT_P)

    ms = jnp.sum(zq * zq, axis=-1, keepdims=True) * (1.0 / Q_RANK)
    qn = (zq * lax.rsqrt(ms + EPS) * gq_ref[0]).astype(BF16)
    rope4 = [jnp.concatenate([t] * HEADS, axis=1) for t in rope]
    q = _rope(_dot(qn, wuq_ref[0]), *rope4)
    qt_ref[0] = (q * q_scale).T.astype(BF16)

    zf = proj(CAT_F, CAT_G)
    zp_ref[0] = proj(CAT_P, CAT_Q)

    zab_ref[0] = _dot(zf.astype(BF16), dftc_ref[...]).astype(BF16)

    gz = _gelu_tanh(zg)
    u = gz[:, :GROUP_W]
    vv = gz[:, GROUP_W:]
    ms = _group_mean(vv * vv, ones_ref[...])
    vn = (vv * lax.rsqrt(ms + EPS) * gsgu_ref[0]).astype(BF16)
    lane = lax.broadcasted_iota(jnp.int32, (SGU_CHUNK, GROUP_W), 1)
    even_head = (lane // HEAD_W) % 2 == 0
    zero = jnp.zeros((SGU_CHUNK, GROUP_W), BF16)
    for c in range(tm // SGU_CHUNK):
        rows = slice(c * SGU_CHUNK, (c + 1) * SGU_CHUNK)
        stacked = jnp.concatenate([jnp.where(even_head, vn[rows], zero), jnp.where(even_head, zero, vn[rows])], axis=0)
        r = _dot(wsgu_ref[0], stacked)
        sel = jnp.where(lane < 2 * HEAD_W, r[:SGU_CHUNK], r[SGU_CHUNK:])
        sgu_ref[0, rows, :] = (u[rows] * (sel + bsgu_ref[0])).astype(BF16)


def _inproj(h, mods, mod_row, rope, p, layer, *, kv_only, tm):
    b, l, d = h.shape
    names = ["w_cat", "dft_c", "g_q", "w_uq", "g_kv", "w_uk", "w_uv", "v_one", "g_sgu", "ones_bd", "w_sgu", "b_sgu"]
    weights = [p[n] for n in names]
    w_specs = [_const_spec(p[n].shape) if n in SHARED_PARAMS else _layer_spec(p[n].shape[1:], layer) for n in names]
    rope_spec = pl.BlockSpec((tm, HEAD_PAD), lambda i, j: (j, 0))
    tok = lambda w, dt: jax.ShapeDtypeStruct((b, l, w), dt)
    tok_spec = lambda w: pl.BlockSpec((1, tm, w), lambda i, j: (i, j, 0))
    tr = lambda w: jax.ShapeDtypeStruct((b, w, l), BF16)
    tr_spec = lambda w: pl.BlockSpec((1, w, tm), lambda i, j: (i, 0, j))
    if kv_only:
        out_shape = [tok(HEADS * HEAD_PAD, BF16), tr(HEADS * V_PAD)]
        out_specs = [tok_spec(HEADS * HEAD_PAD), tr_spec(HEADS * V_PAD)]
    else:
        out_shape = [tok(2 * GROUP_W, BF16), tr(HEADS * HEAD_PAD), tok(HEADS * HEAD_PAD, BF16),
                     tr(HEADS * V_PAD), tok(GROUP_W, BF16), tok(GROUP_W, F32)]
        out_specs = [tok_spec(2 * GROUP_W), tr_spec(HEADS * HEAD_PAD), tok_spec(HEADS * HEAD_PAD),
                     tr_spec(HEADS * V_PAD), tok_spec(GROUP_W), tok_spec(GROUP_W)]
    kern = functools.partial(_inproj_kernel, kv_only=kv_only, tm=tm,
                             q_scale=float(QK_NOPE + QK_ROPE) ** -0.5 * math.log2(math.e))
    return pl.pallas_call(
        kern,
        out_shape=out_shape,
        grid=(b, l // tm),
        in_specs=[
            pl.BlockSpec((1, tm, d), lambda i, j: (i, j, 0)),
            _mod_spec(mods, mod_row),
            _layer_spec((1, d), layer),
            rope_spec, rope_spec, rope_spec,
        ] + w_specs,
        out_specs=out_specs,
        compiler_params=_cparams("parallel", "parallel"),
        name="inproj",
    )(h, mods, p["g_mix"], *rope, *weights)


def _dft_kernel(c_ref, s_ref, z_ref, o_ref):
    za = z_ref[0, :, :GROUP_W]
    zb = z_ref[0, :, GROUP_W:]
    o_ref[0] = (_dot(c_ref[...], za) - _dot(s_ref[...], zb)).astype(BF16)


def _dft(zab, cmat, smat, *, tm):
    b, l, _ = zab.shape
    return pl.pallas_call(
        _dft_kernel,
        out_shape=jax.ShapeDtypeStruct((b, l, GROUP_W), BF16),
        grid=(l // tm, b),
        in_specs=[
            pl.BlockSpec((tm, l), lambda i, j: (i, 0)),
            pl.BlockSpec((tm, l), lambda i, j: (i, 0)),
            pl.BlockSpec((1, l, 2 * GROUP_W), lambda i, j: (j, 0, 0)),
        ],
        out_specs=pl.BlockSpec((1, tm, GROUP_W), lambda i, j: (j, i, 0)),
        compiler_params=_cparams("parallel", "parallel"),
        name="dft",
    )(cmat, smat, zab)


def _dft_fold_kernel(cq_ref, sq_ref, rev_ref, z_ref, o_ref, fold_ref, t_ref, *, l):
    half = l // 2
    nb = half // FLIP
    rev = rev_ref[...]
    fold_ref[0:8, :] = jnp.zeros((8, 2 * GROUP_W), F32)
    for jb in range(nb):
        blk = z_ref[0, (2 * nb - 1 - jb) * FLIP:(2 * nb - jb) * FLIP, :]
        fold_ref[8 + jb * FLIP:8 + (jb + 1) * FLIP, :] = _dot(rev, blk)
    zlo = z_ref[0, 0:half, :].astype(F32)
    mirrored = fold_ref[7:7 + half, :]
    even = (zlo[:, :GROUP_W] + mirrored[:, :GROUP_W]).astype(BF16)
    odd = (zlo[:, GROUP_W:] - mirrored[:, GROUP_W:]).astype(BF16)
    p = _dot(cq_ref[...], even)
    q = _dot(sq_ref[...], odd)
    k_idx = lax.broadcasted_iota(jnp.int32, (p.shape[0], 1), 0)
    sign = (1 - 2 * (k_idx & 1)).astype(F32) * (1.0 / math.sqrt(l))
    p = p + sign * z_ref[0, half:half + 1, :GROUP_W].astype(F32)
    o_ref[0, 0:half, :] = (p[:half] - q[:half]).astype(BF16)
    t_ref[...] = p + q
    upper = t_ref[1:half + 1, :].astype(BF16)
    for jb in range(nb):
        blk = upper[(nb - 1 - jb) * FLIP:(nb - jb) * FLIP, :]
        o_ref[0, half + jb * FLIP:half + (jb + 1) * FLIP, :] = _dot(rev, blk).astype(BF16)


def _dft_fold(zab, cq, sq, rev):
    b, l, _ = zab.shape
    half = l // 2
    rows = half + 16
    return pl.pallas_call(
        functools.partial(_dft_fold_kernel, l=l),
        out_shape=jax.ShapeDtypeStruct((b, l, GROUP_W), BF16),
        grid=(b,),
        in_specs=[
            _const_spec((rows, half)),
            _const_spec((rows, half)),
            _const_spec((FLIP, FLIP)),
            pl.BlockSpec((1, l, 2 * GROUP_W), lambda i: (i, 0, 0)),
        ],
        out_specs=pl.BlockSpec((1, l, GROUP_W), lambda i: (i, 0, 0)),
        scratch_shapes=[pltpu.VMEM((half + 8, 2 * GROUP_W), F32), pltpu.VMEM((rows, GROUP_W), F32)],
        compiler_params=_cparams("parallel"),
        name="dft_fold",
    )(cq, sq, rev, zab)


def _attn_kernel(qt_ref, *refs, seg_lens):
    n_seg = len(seg_lens)
    kv_refs = refs[:2 * n_seg]
    o_ref = refs[2 * n_seg]
    acc_ref = refs[2 * n_seg + 1]
    tq = qt_ref.shape[2]
    chains = [(qs, hd) for qs in range(0, tq, TQ_SUB) for hd in range(HEADS)]
    items = []
    for s, lk in enumerate(seg_lens):
        chunk = min(KEY_CHUNK, lk)
        for c0 in range(0, lk, chunk):
            items += [(s, c0, chunk, ch) for ch in range(len(chains))]
    m = [None] * len(chains)
    acc = [None] * len(chains)
    scores = {}
    for i in range(len(items) + SCORE_LOOKAHEAD):
        if i < len(items):
            s, c0, chunk, ch = items[i]
            qs, hd = chains[ch]
            k = kv_refs[2 * s][0, c0:c0 + chunk, hd * HEAD_PAD:(hd + 1) * HEAD_PAD]
            scores[i] = _dot(k, qt_ref[0, hd * HEAD_PAD:(hd + 1) * HEAD_PAD, qs:qs + TQ_SUB])
        j = i - SCORE_LOOKAHEAD
        if j < 0:
            continue
        s, c0, chunk, ch = items[j]
        qs, hd = chains[ch]
        sc = scores.pop(j)
        cm = sc.max(axis=0, keepdims=True)
        m_new = cm if m[ch] is None else jnp.maximum(m[ch], cm)
        p = jnp.exp2(sc - m_new).astype(BF16)
        vt = kv_refs[2 * s + 1][0, hd * V_PAD:(hd + 1) * V_PAD, c0:c0 + chunk]
        pv = _dot(vt, p)
        acc[ch] = pv if acc[ch] is None else acc[ch] * jnp.exp2(m[ch] - m_new) + pv
        m[ch] = m_new
    for ch, (qs, hd) in enumerate(chains):
        inv = 1.0 / acc[ch][HEAD_W:HEAD_W + 1]
        acc_ref[hd * HEAD_W:(hd + 1) * HEAD_W, qs:qs + TQ_SUB] = acc[ch][:HEAD_W] * inv
    o_ref[0] = acc_ref[...].T.astype(BF16)


def _attention(qt, segments, *, tq):
    b, _, l = qt.shape
    in_specs = [pl.BlockSpec((1, HEADS * HEAD_PAD, tq), lambda i, j: (i, 0, j))]
    args = [qt]
    for k, vt in segments:
        lk = k.shape[1]
        in_specs.append(pl.BlockSpec((1, lk, HEADS * HEAD_PAD), lambda i, j: (i, 0, 0)))
        in_specs.append(pl.BlockSpec((1, HEADS * V_PAD, lk), lambda i, j: (i, 0, 0)))
        args += [k, vt]
    return pl.pallas_call(
        functools.partial(_attn_kernel, seg_lens=tuple(k.shape[1] for k, _ in segments)),
        out_shape=jax.ShapeDtypeStruct((b, l, GROUP_W), BF16),
        grid=(b, l // tq),
        in_specs=in_specs,
        out_specs=pl.BlockSpec((1, tq, GROUP_W), lambda i, j: (i, j, 0)),
        scratch_shapes=[pltpu.VMEM((GROUP_W, tq), F32)],
        compiler_params=_cparams("parallel", "parallel"),
        name="attention",
    )(*args)


def _block_diag(blocks):
    depth, n, r, c = blocks.shape
    rows = [jnp.concatenate([blocks[:, i] if j == i else jnp.zeros((depth, r, c), blocks.dtype)
                             for j in range(n)], axis=2) for i in range(n)]
    return jnp.concatenate(rows, axis=1)


def _channel_dft():
    k = np.arange(HEAD_W)
    ang = 2.0 * np.pi * np.outer(k, k) / HEAD_W
    eye = np.eye(HEADS)
    c = np.kron(eye, np.cos(ang)) / math.sqrt(HEAD_W)
    s = np.kron(eye, np.sin(ang)) / math.sqrt(HEAD_W)
    return jnp.asarray(np.concatenate([c, s], axis=1), F32).astype(BF16)


def _token_dft(l):
    idx = jnp.arange(l, dtype=jnp.int32)
    ang = ((idx[:, None] * idx[None, :]) % l).astype(F32) * (2.0 * math.pi / l)
    norm = 1.0 / math.sqrt(l)
    return (jnp.cos(ang) * norm).astype(BF16), (jnp.sin(ang) * norm).astype(BF16)


def _dft_table_kernel(ca_ref, sa_ref, cb_ref, sb_ref, c_ref, s_ref):
    ca, sa = ca_ref[0], sa_ref[0]
    cb, sb = cb_ref[...], sb_ref[...]
    c_ref[...] = (ca * cb - sa * sb).astype(BF16)
    s_ref[...] = (sa * cb + ca * sb).astype(BF16)


def _folded_token_dft(l):
    half = l // 2
    blocks = pl.cdiv(half + 1, TABLE_ROWS)
    tt = jnp.arange(half, dtype=jnp.int32)[None, :]
    angle = lambda kk: ((kk[:, None] * tt) % l).astype(F32) * (2.0 * math.pi / l)
    coarse = angle(jnp.arange(blocks, dtype=jnp.int32) * TABLE_ROWS)
    fine = angle(jnp.arange(TABLE_ROWS, dtype=jnp.int32))
    norm = 1.0 / math.sqrt(l)
    row_spec = pl.BlockSpec((1, 1, half), lambda i: (i, 0, 0))
    out_spec = pl.BlockSpec((TABLE_ROWS, half), lambda i: (i, 0))
    table = jax.ShapeDtypeStruct((blocks * TABLE_ROWS, half), BF16)
    cq, sq = pl.pallas_call(
        _dft_table_kernel,
        out_shape=[table, table],
        grid=(blocks,),
        in_specs=[row_spec, row_spec, _const_spec((TABLE_ROWS, half)), _const_spec((TABLE_ROWS, half))],
        out_specs=[out_spec, out_spec],
        compiler_params=_cparams("parallel"),
        name="dft_tables",
    )(jnp.cos(coarse)[:, None, :], jnp.sin(coarse)[:, None, :], jnp.cos(fine) * norm, jnp.sin(fine) * norm)
    rev = jnp.asarray(np.eye(FLIP)[::-1].copy(), F32).astype(BF16)
    return cq, sq, rev


def _rope_tables(l):
    pos = jnp.arange(l)
    row = (pos // GRID_W).astype(F32)
    col = (pos % GRID_W).astype(F32)
    inv = jnp.power(ROPE_BASE, -jnp.arange(0, AXIS_ROPE, 2, dtype=F32) / AXIS_ROPE)
    zero = jnp.zeros((l, ROPE_PAIR), F32)
    row_sin, col_sin = jnp.sin(row[:, None] * inv), jnp.sin(col[:, None] * inv)
    row_cos, col_cos = jnp.cos(row[:, None] * inv), jnp.cos(col[:, None] * inv)
    place = lambda base, parts: base.at[:, ROPE_LANE0:ROPE_LANE0 + QK_ROPE].set(jnp.concatenate(parts, axis=1))
    cos = place(jnp.ones((l, HEAD_PAD), F32), [row_cos, row_cos, col_cos, col_cos])
    sin_lo = place(jnp.zeros((l, HEAD_PAD), F32), [-row_sin, zero, -col_sin, zero])
    sin_hi = place(jnp.zeros((l, HEAD_PAD), F32), [zero, row_sin, zero, col_sin])
    return cos, sin_lo, sin_hi


def _identity_rope_tables(l):
    zero = jnp.zeros((l, HEAD_PAD), F32)
    return jnp.ones((l, HEAD_PAD), F32), zero, zero


SHARED_PARAMS = ("dft_c", "v_one", "ones_bd")


def _prep_params(g_ffn1, g_mix, w_in, w_fnet, g_q, w_uq, g_kv, w_ukv,
                 g_sgu, w_sgu, b_sgu, w_pool, s_pool, g_ffn2):
    depth = w_in.shape[0]
    zeros = lambda *shape: jnp.zeros((depth,) + shape, F32)
    p = {}
    p["g_1"] = g_ffn1.reshape(depth, 1, D_MODEL)
    p["g_2"] = g_ffn2.reshape(depth, 1, D_MODEL)
    p["g_mix"] = g_mix.reshape(depth, 1, D_MODEL)
    p["w_cat"] = jnp.concatenate([
        w_in[..., OFF_F:OFF_Q], w_in[..., OFF_G:OFF_P], w_in[..., OFF_P:],
        w_in[..., OFF_Q:OFF_KV], zeros(D_MODEL, Q_PAD - Q_RANK), w_in[..., OFF_KV:OFF_KR],
        zeros(D_MODEL, ROPE_LANE0), w_in[..., OFF_KR:OFF_G],
        zeros(D_MODEL, HEAD_PAD - ROPE_LANE0 - QK_ROPE)], axis=2).astype(BF16)
    p["dft_c"] = _channel_dft()
    p["g_q"] = jnp.concatenate([g_q, zeros(Q_PAD - Q_RANK)], axis=1).reshape(depth, 1, Q_PAD)
    wuq = w_uq.reshape(depth, Q_RANK, HEADS, QK_NOPE + QK_ROPE)
    wuq = jnp.concatenate([wuq, zeros(Q_RANK, HEADS, HEAD_PAD - QK_NOPE - QK_ROPE)], axis=-1)
    p["w_uq"] = jnp.concatenate([wuq.reshape(depth, Q_RANK, HEADS * HEAD_PAD),
                                 zeros(Q_PAD - Q_RANK, HEADS * HEAD_PAD)], axis=1).astype(BF16)
    p["g_kv"] = g_kv.reshape(depth, 1, KV_RANK)
    wukv = w_ukv.reshape(depth, KV_RANK, HEADS, QK_NOPE + HEAD_W)
    p["w_uk"] = jnp.concatenate([wukv[..., :QK_NOPE], zeros(KV_RANK, HEADS, HEAD_PAD - QK_NOPE)],
                                axis=-1).reshape(depth, KV_RANK, HEADS * HEAD_PAD).astype(BF16)
    p["w_uv"] = jnp.concatenate([wukv[..., QK_NOPE:], zeros(KV_RANK, HEADS, V_PAD - HEAD_W)],
                                axis=-1).reshape(depth, KV_RANK, HEADS * V_PAD).astype(BF16)
    v_one = np.zeros((1, HEADS, V_PAD), np.float32)
    v_one[:, :, HEAD_W] = 1.0
    p["v_one"] = jnp.asarray(v_one.reshape(1, HEADS * V_PAD))
    p["g_sgu"] = g_sgu.reshape(depth, 1, GROUP_W)
    p["ones_bd"] = jnp.asarray(np.kron(np.eye(HEADS), np.full((HEAD_W, HEAD_W), 1.0 / HEAD_W)), BF16)
    p["w_sgu"] = jnp.concatenate([jnp.concatenate([w_sgu[:, 0], w_sgu[:, 1]], axis=2),
                                  jnp.concatenate([w_sgu[:, 2], w_sgu[:, 3]], axis=2)], axis=1).astype(BF16)
    p["b_sgu"] = jnp.repeat(jnp.swapaxes(b_sgu, 1, 2), HEAD_W, axis=2)
    p["wf_bd"] = _block_diag(w_fnet).astype(BF16)
    p["wp_bd"] = _block_diag(w_pool).astype(BF16)
    p["s_pool"] = s_pool.reshape(depth, 1, GROUP_W)
    return p


def _mix(h, mods, mod_row, rope, dft, kv_extra, p, layer, *, tm, tq):
    zab, q, k, v, sgu, zp = _inproj(h, mods, mod_row, rope, p, layer, kv_only=False, tm=tm)
    if len(dft) == 3:
        fre = _dft_fold(zab, *dft)
    else:
        fre = _dft(zab, dft[0], dft[1], tm=min(512, h.shape[1]))
    att = _attention(q, [(k, v)] + kv_extra, tq=tq)
    return (fre, att, sgu, zp), (k, v)


def kernel(x, c, ctx, c_ctx, w_ada, b_ada, g_ffn1, w13_ffn1, w2_ffn1, g_mix, w_in, w_fnet, g_q, w_uq, g_kv, w_ukv, g_sgu, w_sgu, b_sgu, w_pool, s_pool, w_out, g_ffn2, w13_ffn2, w2_ffn2, g_final):
    b, l, d = x.shape
    lc = ctx.shape[1]
    depth = w_ada.shape[0]
    tm = 1024
    tm_ffn = 1024
    tmc = min(lc, 256)

    cond_rows = 16
    cond = jnp.concatenate([c, c_ctx[None], jnp.zeros((cond_rows - b - 1, d), F32)], axis=0)
    mods = _ada(cond, w_ada, b_ada).reshape(depth * cond_rows, N_MOD, d)

    rope = _rope_tables(l)
    rope_c = _identity_rope_tables(lc)
    dft = _folded_token_dft(l)
    dft_c = _token_dft(lc)

    w13_1, w2_1 = w13_ffn1.astype(BF16), w2_ffn1.astype(BF16)
    w13_2, w2_2 = w13_ffn2.astype(BF16), w2_ffn2.astype(BF16)
    w_out_b = w_out.astype(BF16)

    flat = lambda a: a.reshape(1, b * lc, a.shape[-1])
    tmc_ffn = min(b * lc, tm_ffn)

    p = _prep_params(g_ffn1, g_mix, w_in, w_fnet, g_q, w_uq, g_kv, w_ukv,
                     g_sgu, w_sgu, b_sgu, w_pool, s_pool, g_ffn2)

    h, hc = x, ctx
    for i in range(depth):
        last = i == depth - 1
        m = (i * cond_rows, True)
        mc = (i * cond_rows + b, False)

        h = _ffn(h, mods, m, p["g_1"], w13_1, w2_1, i, row0=0, tm=tm_ffn)
        hc = _ffn(flat(hc), mods, mc, p["g_1"], w13_1, w2_1, i, row0=0, tm=tmc_ffn).reshape(b, lc, d)

        if last:
            kc, vc = _inproj(hc, mods, mc, rope_c, p, i, kv_only=True, tm=tmc)
        else:
            mix_c, (kc, vc) = _mix(hc, mods, mc, rope_c, dft_c, [], p, i, tm=tmc, tq=tmc)
        mix, _ = _mix(h, mods, m, rope, dft, [(kc, vc)], p, i, tm=tm, tq=TQ)
        mix_w = (p["wf_bd"], w_out_b, p["wp_bd"], p["s_pool"])
        h = _ffn(h, mods, m, p["g_2"], w13_2, w2_2, i, row0=6, tm=tm_ffn, mix=mix + mix_w, mix_seq=l,
                 g_final=g_final.reshape(1, d) if last else None)
        if not last:
            mix_c = tuple(flat(a) for a in mix_c) + mix_w
            hc = _ffn(flat(hc), mods, mc, p["g_2"], w13_2, w2_2, i, row0=6, tm=tmc_ffn,
                      mix=mix_c, mix_seq=lc).reshape(b, lc, d)
    return h
```

```python
import functools
import math

import numpy as np
import jax
import jax.numpy as jnp
from jax import lax
from jax.experimental import pallas as pl
from jax.experimental.pallas import tpu as pltpu

F32 = jnp.float32
BF16 = jnp.bfloat16

D_MODEL = 1024
GRID_W = 64
EPS = 1e-6
N_MOD = 9
GROUP_W = 256
HEADS = 4
HEAD_W = GROUP_W // HEADS
QK_NOPE = 64
QK_ROPE = 32
AXIS_ROPE = QK_ROPE // 2
Q_RANK = 192
KV_RANK = 128
ROPE_BASE = 10000.0
SGU_CHUNK = 128
POOL_WINDOWS = (2, 4, 8, 16)
D_FF = 2816
HEAD_PAD = 128
V_PAD = 96
KEY_CHUNK = 256
SCORE_LOOKAHEAD = 12
FLIP = 256
TABLE_ROWS = 128
ROPE_LANE0 = QK_NOPE

OFF_F = 0
OFF_Q = OFF_F + GROUP_W
OFF_KV = OFF_Q + Q_RANK
OFF_KR = OFF_KV + KV_RANK
OFF_G = OFF_KR + QK_ROPE
OFF_P = OFF_G + 2 * GROUP_W

CAT_F = 0
CAT_G = CAT_F + GROUP_W
CAT_P = CAT_G + 2 * GROUP_W
CAT_Q = CAT_P + GROUP_W
Q_PAD = 256
CAT_KV = CAT_Q + Q_PAD
CAT_KR = CAT_KV + KV_RANK
CAT_W = CAT_KR + HEAD_PAD
ROPE_PAIR = AXIS_ROPE // 2
TQ = 1024
TQ_SUB = 256

MAX_HALF_WINDOW = max(POOL_WINDOWS) // 2
POOL_BLOCK = 256
VMEM_LIMIT = 56 * 1024 * 1024


def _cparams(*sem):
    return pltpu.CompilerParams(dimension_semantics=sem, vmem_limit_bytes=VMEM_LIMIT)


def _const_spec(shape):
    nd = len(shape)
    return pl.BlockSpec(shape, lambda *_: (0,) * nd, pipeline_mode=pl.Buffered(1))


def _layer_spec(shape, layer):
    nd = len(shape)
    return pl.BlockSpec((1,) + tuple(shape), lambda *_: (layer,) + (0,) * nd, pipeline_mode=pl.Buffered(1))


def _mod_spec(mods, mod_row):
    base, per_batch = mod_row
    return pl.BlockSpec((1,) + mods.shape[1:], lambda i, j: (base + i if per_batch else base, 0, 0))


def _dot(a, b):
    return jnp.dot(a, b, preferred_element_type=F32)


def _rms(x):
    return x * lax.rsqrt(jnp.mean(x * x, axis=-1, keepdims=True) + EPS)


def _norm_mod(x, g, shift, scale):
    return (_rms(x) * g) * (1.0 + scale) + shift


def _ada_kernel(c_ref, w_ref, b_ref, o_ref):
    c = c_ref[...]
    s = c * jax.nn.sigmoid(c)
    hi = s.astype(BF16)
    lo = (s - hi.astype(F32)).astype(BF16)
    r = _dot(jnp.concatenate([hi, lo], axis=0), w_ref[0].astype(BF16))
    rows = c.shape[0]
    o_ref[0] = r[:rows] + r[rows:] + b_ref[0]


def _ada(cond, w_ada, b_ada):
    rows, d = cond.shape
    depth, _, width = w_ada.shape
    bn = 2304
    return pl.pallas_call(
        _ada_kernel,
        out_shape=jax.ShapeDtypeStruct((depth, rows, width), F32),
        grid=(depth, width // bn),
        in_specs=[
            pl.BlockSpec((rows, d), lambda i, j: (0, 0)),
            pl.BlockSpec((1, d, bn), lambda i, j: (i, 0, j)),
            pl.BlockSpec((1, 1, bn), lambda i, j: (i, 0, j)),
        ],
        out_specs=pl.BlockSpec((1, rows, bn), lambda i, j: (i, 0, j)),
        compiler_params=_cparams("parallel", "parallel"),
        name="ada",
    )(cond, w_ada, b_ada.reshape(depth, 1, width))


def _pooled_rows(pad_ref, base, t0, seq_len, clip):
    halo = MAX_HALF_WINDOW
    rb = POOL_BLOCK
    n = rb + 2 * halo
    lane = lax.broadcasted_iota(jnp.int32, (rb, HEAD_PAD), 1)
    first_group = lane < HEAD_W

    def up(a, kk):
        return pltpu.roll(a, n - kk, axis=0)

    def down(a, kk):
        return pltpu.roll(a, kk, axis=0)

    pooled = []
    for tile in range(GROUP_W // HEAD_PAD):
        w_a, w_b = POOL_WINDOWS[2 * tile], POOL_WINDOWS[2 * tile + 1]
        x = pad_ref[base:base + n, tile * HEAD_PAD:(tile + 1) * HEAD_PAD]
        run = {1: x}
        width = 1
        while width < min(w_b, halo):
            run[2 * width] = run[width] + up(run[width], width)
            width *= 2

        def window(w):
            if w in run:
                return down(run[w], w // 2)
            return down(run[w // 2], w // 2) + run[w // 2]

        win = jnp.where(first_group, window(w_a)[halo:halo + rb], window(w_b)[halo:halo + rb])
        z = x[halo:halo + rb]
        if clip:
            t = lax.broadcasted_iota(jnp.int32, (rb, HEAD_PAD), 0) + t0
            hw = jnp.where(first_group, w_a // 2, w_b // 2)
            cnt = jnp.minimum(t + hw, seq_len) - jnp.maximum(t - hw, 0)
            pooled.append(win / cnt.astype(F32) - z)
        else:
            pooled.append(win * jnp.where(first_group, 1.0 / w_a, 1.0 / w_b) - z)
    return jnp.concatenate(pooled, axis=1)


def _pool_tile(zp_ref, prev_ref, next_ref, pad_ref, seq_len):
    halo = MAX_HALF_WINDOW
    tm = zp_ref.shape[1]
    zeros = jnp.zeros((halo, GROUP_W), F32)
    blocks = []
    if seq_len >= tm:
        j = pl.program_id(1)
        pad_ref[0:halo, :] = jnp.where(j == 0, zeros, prev_ref[0])
        pad_ref[halo:halo + tm, :] = zp_ref[0]
        pad_ref[halo + tm:2 * halo + tm, :] = jnp.where(j == seq_len // tm - 1, zeros, next_ref[0])
        n_blocks, n_tiles = tm // POOL_BLOCK, seq_len // tm
        for r in range(n_blocks):
            rows = functools.partial(_pooled_rows, pad_ref, r * POOL_BLOCK, j * tm + r * POOL_BLOCK, seq_len)
            at_end = ([j == 0] if r == 0 else []) + ([j == n_tiles - 1] if r == n_blocks - 1 else [])
            if at_end:
                pred = at_end[0] if len(at_end) == 1 else jnp.logical_or(*at_end)
                blocks.append(lax.cond(pred, lambda f=rows: f(True), lambda f=rows: f(False)))
            else:
                blocks.append(rows(False))
    else:
        pad_ref[0:halo, :] = zeros
        pad_ref[halo + POOL_BLOCK:2 * halo + POOL_BLOCK, :] = zeros
        for r in range(tm // POOL_BLOCK):
            pad_ref[halo:halo + POOL_BLOCK, :] = zp_ref[0, r * POOL_BLOCK:(r + 1) * POOL_BLOCK, :]
            blocks.append(_pooled_rows(pad_ref, 0, 0, seq_len, True))
    return jnp.concatenate(blocks, axis=0)


def _ffn_kernel(h_ref, mod_ref, g_ref, w13_ref, w2_ref, *rest, row0, fchunk, mix_seq, final_norm):
    rest = list(rest)
    mix_refs = [rest.pop(0) for _ in range(10)] if mix_seq else None
    gf_ref = rest.pop(0) if final_norm else None
    o_ref, xn_ref, act_ref = rest[:3]
    shift = mod_ref[0, row0:row0 + 1, :]
    scale = mod_ref[0, row0 + 1:row0 + 2, :]
    gate = mod_ref[0, row0 + 2:row0 + 3, :]
    x = h_ref[0]
    if mix_seq:
        f_ref, att_ref, sgu_ref, zp_ref, zprev_ref, znext_ref, wf_ref, wo_ref, wp_ref, sp_ref = mix_refs
        pooled = _pool_tile(zp_ref, zprev_ref, znext_ref, rest[3], mix_seq)
        pool = (_dot(pooled.astype(BF16), wp_ref[0]) * sp_ref[0]).astype(BF16)
        fmix = _dot(f_ref[0], wf_ref[0]).astype(BF16)
        mixed = jnp.concatenate([fmix, att_ref[0], sgu_ref[0], pool], axis=1)
        x = x + mod_ref[0, 5:6, :] * _dot(mixed, wo_ref[0])
    xn_ref[...] = _norm_mod(x, g_ref[0], shift, scale).astype(BF16)
    for j in range(D_FF // fchunk):
        sl = slice(j * fchunk, (j + 1) * fchunk)
        a = _dot(xn_ref[...], w13_ref[0, :, sl])
        b = _dot(xn_ref[...], w13_ref[0, :, D_FF + j * fchunk:D_FF + (j + 1) * fchunk])
        act_ref[:, sl] = (a * jax.nn.sigmoid(a) * b).astype(BF16)
    y = _dot(act_ref[...], w2_ref[0])
    out = x + (0.5 * gate) * y
    if final_norm:
        out = _rms(out) * gf_ref[...]
    o_ref[0] = out


def _ffn(h, mods, mod_row, g, w13, w2, layer, *, row0, tm, mix=None, mix_seq=0, g_final=None):
    b, l, d = h.shape
    kern = functools.partial(_ffn_kernel, row0=row0, fchunk=256,
                             mix_seq=mix_seq if mix is not None else 0, final_norm=g_final is not None)
    extra, extra_specs, scratch = [], [], []
    if mix is not None:
        fre, att, sgu, zp, wf_bd, w_out, wp_bd, s_pool = mix
        assert mix_seq % tm == 0 or (mix_seq == POOL_BLOCK and tm % POOL_BLOCK == 0)
        halo = MAX_HALF_WINDOW
        per_tile, last = tm // halo, l // halo - 1
        group = pl.BlockSpec((1, tm, GROUP_W), lambda i, j: (i, j, 0))
        prev_rows = pl.BlockSpec((1, halo, GROUP_W), lambda i, j: (i, jnp.maximum(j * per_tile - 1, 0), 0))
        next_rows = pl.BlockSpec((1, halo, GROUP_W), lambda i, j: (i, jnp.minimum((j + 1) * per_tile, last), 0))
        extra += [fre, att, sgu, zp, zp, zp, wf_bd, w_out, wp_bd, s_pool]
        extra_specs += [group, group, group, group, prev_rows, next_rows,
                        _layer_spec((GROUP_W, GROUP_W), layer), _layer_spec((4 * GROUP_W, d), layer),
                        _layer_spec((GROUP_W, GROUP_W), layer), _layer_spec((1, GROUP_W), layer)]
        scratch = [pltpu.VMEM((tm + 2 * halo, GROUP_W), F32)]
    if g_final is not None:
        extra.append(g_final)
        extra_specs.append(_const_spec((1, d)))
    return pl.pallas_call(
        kern,
        out_shape=jax.ShapeDtypeStruct(h.shape, F32),
        grid=(b, l // tm),
        in_specs=[
            pl.BlockSpec((1, tm, d), lambda i, j: (i, j, 0)),
            _mod_spec(mods, mod_row),
            _layer_spec((1, d), layer),
            _layer_spec((d, 2 * D_FF), layer),
            _layer_spec((D_FF, d), layer),
        ] + extra_specs,
        out_specs=pl.BlockSpec((1, tm, d), lambda i, j: (i, j, 0)),
        scratch_shapes=[pltpu.VMEM((tm, d), BF16), pltpu.VMEM((tm, D_FF), BF16)] + scratch,
        compiler_params=_cparams("parallel", "parallel"),
        name="ffn",
    )(h, mods, g, w13, w2, *extra)


def _gelu_tanh(x):
    return 0.5 * x * (1.0 + jnp.tanh(0.7978845608028654 * (x + 0.044715 * (x * x * x))))


def _group_mean(x, ones_bd):
    hi = x.astype(BF16)
    lo = (x - hi.astype(F32)).astype(BF16)
    return _dot(hi, ones_bd) + _dot(lo, ones_bd)


def _rope(x, cos, sin_lo, sin_hi):
    n = x.shape[1]
    up = pltpu.roll(x, n - ROPE_PAIR, axis=1)
    down = pltpu.roll(x, ROPE_PAIR, axis=1)
    return x * cos + up * sin_lo + down * sin_hi


def _inproj_kernel(h_ref, mod_ref, g_ref, cos_ref, slo_ref, shi_ref,
                   wcat_ref, dftc_ref, gq_ref, wuq_ref, gkv_ref, wuk_ref, wuv_ref, vone_ref,
                   gsgu_ref, ones_ref, wsgu_ref, bsgu_ref,
                   *out_refs, kv_only, tm, q_scale):
    if kv_only:
        k_ref, vt_ref = out_refs
    else:
        zab_ref, qt_ref, k_ref, vt_ref, sgu_ref, zp_ref = out_refs
    shift = mod_ref[0, 3:4, :]
    scale = mod_ref[0, 4:5, :]
    n = _norm_mod(h_ref[0], g_ref[0], shift, scale).astype(BF16)
    rope = (cos_ref[...], slo_ref[...], shi_ref[...])

    def proj(c0, c1):
        return _dot(n, wcat_ref[0, :, c0:c1])

    zkv = proj(CAT_KV, CAT_W)
    if not kv_only:
        zq = proj(CAT_Q, CAT_KV)

    kvn = (_rms(zkv[:, :KV_RANK]) * gkv_ref[0]).astype(BF16)
    k = _dot(kvn, wuk_ref[0]) + jnp.concatenate([_rope(zkv[:, KV_RANK:], *rope)] * HEADS, axis=1)
    k_ref[0] = k.astype(BF16)
    vt_ref[0] = (_dot(kvn, wuv_ref[0]) + vone_ref[...]).T.astype(BF16)
    if kv_only:
        return

    zg = proj(CAT_G, CAT_P)

    ms = jnp.sum(zq * zq, axis=-1, keepdims=True) * (1.0 / Q_RANK)
    qn = (zq * lax.rsqrt(ms + EPS) * gq_ref[0]).astype(BF16)
    rope4 = [jnp.concatenate([t] * HEADS, axis=1) for t in rope]
    q = _rope(_dot(qn, wuq_ref[0]), *rope4)
    qt_ref[0] = (q * q_scale).T.astype(BF16)

    zf = proj(CAT_F, CAT_G)
    zp_ref[0] = proj(CAT_P, CAT_Q)

    zab_ref[0] = _dot(zf.astype(BF16), dftc_ref[...]).astype(BF16)

    gz = _gelu_tanh(zg)
    u = gz[:, :GROUP_W]
    vv = gz[:, GROUP_W:]
    ms = _group_mean(vv * vv, ones_ref[...])
    vn = (vv * lax.rsqrt(ms + EPS) * gsgu_ref[0]).astype(BF16)
    lane = lax.broadcasted_iota(jnp.int32, (SGU_CHUNK, GROUP_W), 1)
    even_head = (lane // HEAD_W) % 2 == 0
    zero = jnp.zeros((SGU_CHUNK, GROUP_W), BF16)
    for c in range(tm // SGU_CHUNK):
        rows = slice(c * SGU_CHUNK, (c + 1) * SGU_CHUNK)
        stacked = jnp.concatenate([jnp.where(even_head, vn[rows], zero), jnp.where(even_head, zero, vn[rows])], axis=0)
        r = _dot(wsgu_ref[0], stacked)
        sel = jnp.where(lane < 2 * HEAD_W, r[:SGU_CHUNK], r[SGU_CHUNK:])
        sgu_ref[0, rows, :] = (u[rows] * (sel + bsgu_ref[0])).astype(BF16)


def _inproj(h, mods, mod_row, rope, p, layer, *, kv_only, tm):
    b, l, d = h.shape
    names = ["w_cat", "dft_c", "g_q", "w_uq", "g_kv", "w_uk", "w_uv", "v_one", "g_sgu", "ones_bd", "w_sgu", "b_sgu"]
    weights = [p[n] for n in names]
    w_specs = [_const_spec(p[n].shape) if n in SHARED_PARAMS else _layer_spec(p[n].shape[1:], layer) for n in names]
    rope_spec = pl.BlockSpec((tm, HEAD_PAD), lambda i, j: (j, 0))
    tok = lambda w, dt: jax.ShapeDtypeStruct((b, l, w), dt)
    tok_spec = lambda w: pl.BlockSpec((1, tm, w), lambda i, j: (i, j, 0))
    tr = lambda w: jax.ShapeDtypeStruct((b, w, l), BF16)
    tr_spec = lambda w: pl.BlockSpec((1, w, tm), lambda i, j: (i, 0, j))
    if kv_only:
        out_shape = [tok(HEADS * HEAD_PAD, BF16), tr(HEADS * V_PAD)]
        out_specs = [tok_spec(HEADS * HEAD_PAD), tr_spec(HEADS * V_PAD)]
    else:
        out_shape = [tok(2 * GROUP_W, BF16), tr(HEADS * HEAD_PAD), tok(HEADS * HEAD_PAD, BF16),
                     tr(HEADS * V_PAD), tok(GROUP_W, BF16), tok(GROUP_W, F32)]
        out_specs = [tok_spec(2 * GROUP_W), tr_spec(HEADS * HEAD_PAD), tok_spec(HEADS * HEAD_PAD),
                     tr_spec(HEADS * V_PAD), tok_spec(GROUP_W), tok_spec(GROUP_W)]
    kern = functools.partial(_inproj_kernel, kv_only=kv_only, tm=tm,
                             q_scale=float(QK_NOPE + QK_ROPE) ** -0.5 * math.log2(math.e))
    return pl.pallas_call(
        kern,
        out_shape=out_shape,
        grid=(b, l // tm),
        in_specs=[
            pl.BlockSpec((1, tm, d), lambda i, j: (i, j, 0)),
            _mod_spec(mods, mod_row),
            _layer_spec((1, d), layer),
            rope_spec, rope_spec, rope_spec,
        ] + w_specs,
        out_specs=out_specs,
        compiler_params=_cparams("parallel", "parallel"),
        name="inproj",
    )(h, mods, p["g_mix"], *rope, *weights)


def _dft_kernel(c_ref, s_ref, z_ref, o_ref):
    za = z_ref[0, :, :GROUP_W]
    zb = z_ref[0, :, GROUP_W:]
    o_ref[0] = (_dot(c_ref[...], za) - _dot(s_ref[...], zb)).astype(BF16)


def _dft(zab, cmat, smat, *, tm):
    b, l, _ = zab.shape
    return pl.pallas_call(
        _dft_kernel,
        out_shape=jax.ShapeDtypeStruct((b, l, GROUP_W), BF16),
        grid=(l // tm, b),
        in_specs=[
            pl.BlockSpec((tm, l), lambda i, j: (i, 0)),
            pl.BlockSpec((tm, l), lambda i, j: (i, 0)),
            pl.BlockSpec((1, l, 2 * GROUP_W), lambda i, j: (j, 0, 0)),
        ],
        out_specs=pl.BlockSpec((1, tm, GROUP_W), lambda i, j: (j, i, 0)),
        compiler_params=_cparams("parallel", "parallel"),
        name="dft",
    )(cmat, smat, zab)


def _dft_fold_kernel(cq_ref, sq_ref, rev_ref, z_ref, o_ref, fold_ref, t_ref, *, l):
    half = l // 2
    nb = half // FLIP
    rev = rev_ref[...]
    fold_ref[0:8, :] = jnp.zeros((8, 2 * GROUP_W), F32)
    for jb in range(nb):
        blk = z_ref[0, (2 * nb - 1 - jb) * FLIP:(2 * nb - jb) * FLIP, :]
        fold_ref[8 + jb * FLIP:8 + (jb + 1) * FLIP, :] = _dot(rev, blk)
    zlo = z_ref[0, 0:half, :].astype(F32)
    mirrored = fold_ref[7:7 + half, :]
    even = (zlo[:, :GROUP_W] + mirrored[:, :GROUP_W]).astype(BF16)
    odd = (zlo[:, GROUP_W:] - mirrored[:, GROUP_W:]).astype(BF16)
    p = _dot(cq_ref[...], even)
    q = _dot(sq_ref[...], odd)
    k_idx = lax.broadcasted_iota(jnp.int32, (p.shape[0], 1), 0)
    sign = (1 - 2 * (k_idx & 1)).astype(F32) * (1.0 / math.sqrt(l))
    p = p + sign * z_ref[0, half:half + 1, :GROUP_W].astype(F32)
    o_ref[0, 0:half, :] = (p[:half] - q[:half]).astype(BF16)
    t_ref[...] = p + q
    upper = t_ref[1:half + 1, :].astype(BF16)
    for jb in range(nb):
        blk = upper[(nb - 1 - jb) * FLIP:(nb - jb) * FLIP, :]
        o_ref[0, half + jb * FLIP:half + (jb + 1) * FLIP, :] = _dot(rev, blk).astype(BF16)


def _dft_fold(zab, cq, sq, rev):
    b, l, _ = zab.shape
    half = l // 2
    rows = half + 16
    return pl.pallas_call(
        functools.partial(_dft_fold_kernel, l=l),
        out_shape=jax.ShapeDtypeStruct((b, l, GROUP_W), BF16),
        grid=(b,),
        in_specs=[
            _const_spec((rows, half)),
            _const_spec((rows, half)),
            _const_spec((FLIP, FLIP)),
            pl.BlockSpec((1, l, 2 * GROUP_W), lambda i: (i, 0, 0)),
        ],
        out_specs=pl.BlockSpec((1, l, GROUP_W), lambda i: (i, 0, 0)),
        scratch_shapes=[pltpu.VMEM((half + 8, 2 * GROUP_W), F32), pltpu.VMEM((rows, GROUP_W), F32)],
        compiler_params=_cparams("parallel"),
        name="dft_fold",
    )(cq, sq, rev, zab)


def _attn_kernel(qt_ref, *refs, seg_lens):
    n_seg = len(seg_lens)
    kv_refs = refs[:2 * n_seg]
    o_ref = refs[2 * n_seg]
    acc_ref = refs[2 * n_seg + 1]
    tq = qt_ref.shape[2]
    chains = [(qs, hd) for qs in range(0, tq, TQ_SUB) for hd in range(HEADS)]
    items = []
    for s, lk in enumerate(seg_lens):
        chunk = min(KEY_CHUNK, lk)
        for c0 in range(0, lk, chunk):
            items += [(s, c0, chunk, ch) for ch in range(len(chains))]
    m = [None] * len(chains)
    acc = [None] * len(chains)
    scores = {}
    for i in range(len(items) + SCORE_LOOKAHEAD):
        if i < len(items):
            s, c0, chunk, ch = items[i]
            qs, hd = chains[ch]
            k = kv_refs[2 * s][0, c0:c0 + chunk, hd * HEAD_PAD:(hd + 1) * HEAD_PAD]
            scores[i] = _dot(k, qt_ref[0, hd * HEAD_PAD:(hd + 1) * HEAD_PAD, qs:qs + TQ_SUB])
        j = i - SCORE_LOOKAHEAD
        if j < 0:
            continue
        s, c0, chunk, ch = items[j]
        qs, hd = chains[ch]
        sc = scores.pop(j)
        cm = sc.max(axis=0, keepdims=True)
        m_new = cm if m[ch] is None else jnp.maximum(m[ch], cm)
        p = jnp.exp2(sc - m_new).astype(BF16)
        vt = kv_refs[2 * s + 1][0, hd * V_PAD:(hd + 1) * V_PAD, c0:c0 + chunk]
        pv = _dot(vt, p)
        acc[ch] = pv if acc[ch] is None else acc[ch] * jnp.exp2(m[ch] - m_new) + pv
        m[ch] = m_new
    for ch, (qs, hd) in enumerate(chains):
        inv = 1.0 / acc[ch][HEAD_W:HEAD_W + 1]
        acc_ref[hd * HEAD_W:(hd + 1) * HEAD_W, qs:qs + TQ_SUB] = acc[ch][:HEAD_W] * inv
    o_ref[0] = acc_ref[...].T.astype(BF16)


def _attention(qt, segments, *, tq):
    b, _, l = qt.shape
    in_specs = [pl.BlockSpec((1, HEADS * HEAD_PAD, tq), lambda i, j: (i, 0, j))]
    args = [qt]
    for k, vt in segments:
        lk = k.shape[1]
        in_specs.append(pl.BlockSpec((1, lk, HEADS * HEAD_PAD), lambda i, j: (i, 0, 0)))
        in_specs.append(pl.BlockSpec((1, HEADS * V_PAD, lk), lambda i, j: (i, 0, 0)))
        args += [k, vt]
    return pl.pallas_call(
        functools.partial(_attn_kernel, seg_lens=tuple(k.shape[1] for k, _ in segments)),
        out_shape=jax.ShapeDtypeStruct((b, l, GROUP_W), BF16),
        grid=(b, l // tq),
        in_specs=in_specs,
        out_specs=pl.BlockSpec((1, tq, GROUP_W), lambda i, j: (i, j, 0)),
        scratch_shapes=[pltpu.VMEM((GROUP_W, tq), F32)],
        compiler_params=_cparams("parallel", "parallel"),
        name="attention",
    )(*args)


def _block_diag(blocks):
    depth, n, r, c = blocks.shape
    rows = [jnp.concatenate([blocks[:, i] if j == i else jnp.zeros((depth, r, c), blocks.dtype)
                             for j in range(n)], axis=2) for i in range(n)]
    return jnp.concatenate(rows, axis=1)


def _channel_dft():
    k = np.arange(HEAD_W)
    ang = 2.0 * np.pi * np.outer(k, k) / HEAD_W
    eye = np.eye(HEADS)
    c = np.kron(eye, np.cos(ang)) / math.sqrt(HEAD_W)
    s = np.kron(eye, np.sin(ang)) / math.sqrt(HEAD_W)
    return jnp.asarray(np.concatenate([c, s], axis=1), F32).astype(BF16)


def _token_dft(l):
    idx = jnp.arange(l, dtype=jnp.int32)
    ang = ((idx[:, None] * idx[None, :]) % l).astype(F32) * (2.0 * math.pi / l)
    norm = 1.0 / math.sqrt(l)
    return (jnp.cos(ang) * norm).astype(BF16), (jnp.sin(ang) * norm).astype(BF16)


def _dft_table_kernel(ca_ref, sa_ref, cb_ref, sb_ref, c_ref, s_ref):
    ca, sa = ca_ref[0], sa_ref[0]
    cb, sb = cb_ref[...], sb_ref[...]
    c_ref[...] = (ca * cb - sa * sb).astype(BF16)
    s_ref[...] = (sa * cb + ca * sb).astype(BF16)


def _folded_token_dft(l):
    half = l // 2
    blocks = pl.cdiv(half + 1, TABLE_ROWS)
    tt = jnp.arange(half, dtype=jnp.int32)[None, :]
    angle = lambda kk: ((kk[:, None] * tt) % l).astype(F32) * (2.0 * math.pi / l)
    coarse = angle(jnp.arange(blocks, dtype=jnp.int32) * TABLE_ROWS)
    fine = angle(jnp.arange(TABLE_ROWS, dtype=jnp.int32))
    norm = 1.0 / math.sqrt(l)
    row_spec = pl.BlockSpec((1, 1, half), lambda i: (i, 0, 0))
    out_spec = pl.BlockSpec((TABLE_ROWS, half), lambda i: (i, 0))
    table = jax.ShapeDtypeStruct((blocks * TABLE_ROWS, half), BF16)
    cq, sq = pl.pallas_call(
        _dft_table_kernel,
        out_shape=[table, table],
        grid=(blocks,),
        in_specs=[row_spec, row_spec, _const_spec((TABLE_ROWS, half)), _const_spec((TABLE_ROWS, half))],
        out_specs=[out_spec, out_spec],
        compiler_params=_cparams("parallel"),
        name="dft_tables",
    )(jnp.cos(coarse)[:, None, :], jnp.sin(coarse)[:, None, :], jnp.cos(fine) * norm, jnp.sin(fine) * norm)
    rev = jnp.asarray(np.eye(FLIP)[::-1].copy(), F32).astype(BF16)
    return cq, sq, rev


def _rope_tables(l):
    pos = jnp.arange(l)
    row = (pos // GRID_W).astype(F32)
    col = (pos % GRID_W).astype(F32)
    inv = jnp.power(ROPE_BASE, -jnp.arange(0, AXIS_ROPE, 2, dtype=F32) / AXIS_ROPE)
    zero = jnp.zeros((l, ROPE_PAIR), F32)
    row_sin, col_sin = jnp.sin(row[:, None] * inv), jnp.sin(col[:, None] * inv)
    row_cos, col_cos = jnp.cos(row[:, None] * inv), jnp.cos(col[:, None] * inv)
    place = lambda base, parts: base.at[:, ROPE_LANE0:ROPE_LANE0 + QK_ROPE].set(jnp.concatenate(parts, axis=1))
    cos = place(jnp.ones((l, HEAD_PAD), F32), [row_cos, row_cos, col_cos, col_cos])
    sin_lo = place(jnp.zeros((l, HEAD_PAD), F32), [-row_sin, zero, -col_sin, zero])
    sin_hi = place(jnp.zeros((l, HEAD_PAD), F32), [zero, row_sin, zero, col_sin])
    return cos, sin_lo, sin_hi


def _identity_rope_tables(l):
    zero = jnp.zeros((l, HEAD_PAD), F32)
    return jnp.ones((l, HEAD_PAD), F32), zero, zero


SHARED_PARAMS = ("dft_c", "v_one", "ones_bd")


def _prep_params(g_ffn1, g_mix, w_in, w_fnet, g_q, w_uq, g_kv, w_ukv,
                 g_sgu, w_sgu, b_sgu, w_pool, s_pool, g_ffn2):
    depth = w_in.shape[0]
    zeros = lambda *shape: jnp.zeros((depth,) + shape, F32)
    p = {}
    p["g_1"] = g_ffn1.reshape(depth, 1, D_MODEL)
    p["g_2"] = g_ffn2.reshape(depth, 1, D_MODEL)
    p["g_mix"] = g_mix.reshape(depth, 1, D_MODEL)
    p["w_cat"] = jnp.concatenate([
        w_in[..., OFF_F:OFF_Q], w_in[..., OFF_G:OFF_P], w_in[..., OFF_P:],
        w_in[..., OFF_Q:OFF_KV], zeros(D_MODEL, Q_PAD - Q_RANK), w_in[..., OFF_KV:OFF_KR],
        zeros(D_MODEL, ROPE_LANE0), w_in[..., OFF_KR:OFF_G],
        zeros(D_MODEL, HEAD_PAD - ROPE_LANE0 - QK_ROPE)], axis=2).astype(BF16)
    p["dft_c"] = _channel_dft()
    p["g_q"] = jnp.concatenate([g_q, zeros(Q_PAD - Q_RANK)], axis=1).reshape(depth, 1, Q_PAD)
    wuq = w_uq.reshape(depth, Q_RANK, HEADS, QK_NOPE + QK_ROPE)
    wuq = jnp.concatenate([wuq, zeros(Q_RANK, HEADS, HEAD_PAD - QK_NOPE - QK_ROPE)], axis=-1)
    p["w_uq"] = jnp.concatenate([wuq.reshape(depth, Q_RANK, HEADS * HEAD_PAD),
                                 zeros(Q_PAD - Q_RANK, HEADS * HEAD_PAD)], axis=1).astype(BF16)
    p["g_kv"] = g_kv.reshape(depth, 1, KV_RANK)
    wukv = w_ukv.reshape(depth, KV_RANK, HEADS, QK_NOPE + HEAD_W)
    p["w_uk"] = jnp.concatenate([wukv[..., :QK_NOPE], zeros(KV_RANK, HEADS, HEAD_PAD - QK_NOPE)],
                                axis=-1).reshape(depth, KV_RANK, HEADS * HEAD_PAD).astype(BF16)
    p["w_uv"] = jnp.concatenate([wukv[..., QK_NOPE:], zeros(KV_RANK, HEADS, V_PAD - HEAD_W)],
                                axis=-1).reshape(depth, KV_RANK, HEADS * V_PAD).astype(BF16)
    v_one = np.zeros((1, HEADS, V_PAD), np.float32)
    v_one[:, :, HEAD_W] = 1.0
    p["v_one"] = jnp.asarray(v_one.reshape(1, HEADS * V_PAD))
    p["g_sgu"] = g_sgu.reshape(depth, 1, GROUP_W)
    p["ones_bd"] = jnp.asarray(np.kron(np.eye(HEADS), np.full((HEAD_W, HEAD_W), 1.0 / HEAD_W)), BF16)
    p["w_sgu"] = jnp.concatenate([jnp.concatenate([w_sgu[:, 0], w_sgu[:, 1]], axis=2),
                                  jnp.concatenate([w_sgu[:, 2], w_sgu[:, 3]], axis=2)], axis=1).astype(BF16)
    p["b_sgu"] = jnp.repeat(jnp.swapaxes(b_sgu, 1, 2), HEAD_W, axis=2)
    p["wf_bd"] = _block_diag(w_fnet).astype(BF16)
    p["wp_bd"] = _block_diag(w_pool).astype(BF16)
    p["s_pool"] = s_pool.reshape(depth, 1, GROUP_W)
    return p


def _mix(h, mods, mod_row, rope, dft, kv_extra, p, layer, *, tm, tq):
    zab, q, k, v, sgu, zp = _inproj(h, mods, mod_row, rope, p, layer, kv_only=False, tm=tm)
    if len(dft) == 3:
        fre = _dft_fold(zab, *dft)
    else:
        fre = _dft(zab, dft[0], dft[1], tm=min(512, h.shape[1]))
    att = _attention(q, [(k, v)] + kv_extra, tq=tq)
    return (fre, att, sgu, zp), (k, v)


def kernel(x, c, ctx, c_ctx, w_ada, b_ada, g_ffn1, w13_ffn1, w2_ffn1, g_mix, w_in, w_fnet, g_q, w_uq, g_kv, w_ukv, g_sgu, w_sgu, b_sgu, w_pool, s_pool, w_out, g_ffn2, w13_ffn2, w2_ffn2, g_final):
    b, l, d = x.shape
    lc = ctx.shape[1]
    depth = w_ada.shape[0]
    tm = 1024
    tm_ffn = 1024
    tmc = min(lc, 256)

    cond_rows = 16
    cond = jnp.concatenate([c, c_ctx[None], jnp.zeros((cond_rows - b - 1, d), F32)], axis=0)
    mods = _ada(cond, w_ada, b_ada).reshape(depth * cond_rows, N_MOD, d)

    rope = _rope_tables(l)
    rope_c = _identity_rope_tables(lc)
    dft = _folded_token_dft(l)
    dft_c = _token_dft(lc)

    w13_1, w2_1 = w13_ffn1.astype(BF16), w2_ffn1.astype(BF16)
    w13_2, w2_2 = w13_ffn2.astype(BF16), w2_ffn2.astype(BF16)
    w_out_b = w_out.astype(BF16)

    flat = lambda a: a.reshape(1, b * lc, a.shape[-1])
    tmc_ffn = min(b * lc, tm_ffn)

    p = _prep_params(g_ffn1, g_mix, w_in, w_fnet, g_q, w_uq, g_kv, w_ukv,
                     g_sgu, w_sgu, b_sgu, w_pool, s_pool, g_ffn2)

    h, hc = x, ctx
    for i in range(depth):
        last = i == depth - 1
        m = (i * cond_rows, True)
        mc = (i * cond_rows + b, False)

        h = _ffn(h, mods, m, p["g_1"], w13_1, w2_1, i, row0=0, tm=tm_ffn)
        hc = _ffn(flat(hc), mods, mc, p["g_1"], w13_1, w2_1, i, row0=0, tm=tmc_ffn).reshape(b, lc, d)

        if last:
            kc, vc = _inproj(hc, mods, mc, rope_c, p, i, kv_only=True, tm=tmc)
        else:
            mix_c, (kc, vc) = _mix(hc, mods, mc, rope_c, dft_c, [], p, i, tm=tmc, tq=tmc)
        mix, _ = _mix(h, mods, m, rope, dft, [(kc, vc)], p, i, tm=tm, tq=TQ)
        mix_w = (p["wf_bd"], w_out_b, p["wp_bd"], p["s_pool"])
        h = _ffn(h, mods, m, p["g_2"], w13_2, w2_2, i, row0=6, tm=tm_ffn, mix=mix + mix_w, mix_seq=l,
                 g_final=g_final.reshape(1, d) if last else None)
        if not last:
            mix_c = tuple(flat(a) for a in mix_c) + mix_w
            hc = _ffn(flat(hc), mods, mc, p["g_2"], w13_2, w2_2, i, row0=6, tm=tmc_ffn,
                      mix=mix_c, mix_seq=lc).reshape(b, lc, d)
    return h
```
